```python
import math
import jax, jax.numpy as jnp
from jax import lax
import numpy as np

D_MODEL = 2048
BATCH = 2
SEQ = 16384
DEPTH = 4

N_MIXERS = 3
N_A = (DEPTH + 2) // 3
N_B = (DEPTH + 1) // 3
N_C = DEPTH // 3
D_FF = 5632
EPS = 1e-6
NA_HEAD_DIM = 32
NA_HEADS = D_MODEL // NA_HEAD_DIM
GRID_W = 64
WIN_H = 8
WIN_W = 16
CHUNK = 128
D_SGU = 2 * D_MODEL
SGU_GROUPS = 16
SGU_GROUP_DIM = D_SGU // SGU_GROUPS
POOL_WINDOWS = (2, 4, 8, 16)
POOL_GROUPS = len(POOL_WINDOWS)
POOL_GROUP_DIM = D_MODEL // POOL_GROUPS

kernel_name = "hybrid_natten_sgu_pool_macaron"


def rmsnorm(x, g):
    xf = x.astype(jnp.float32)
    y = xf * lax.rsqrt(jnp.mean(xf * xf, axis=-1, keepdims=True) + EPS)
    return (y * g.astype(jnp.float32)).astype(x.dtype)


def swiglu(h, w_gate, w_up, w_down):
    return (jax.nn.silu(h @ w_gate) * (h @ w_up)) @ w_down


def neighbourhood_attention(h, w_qkv, q_gain, k_gain, rpb, w_o):
    B, T, D = h.shape
    rows = T // GRID_W
    kh = min(WIN_H, rows)
    qkv = h @ w_qkv
    q, k, v = jnp.split(qkv, 3, axis=-1)
    shp = (B, rows, GRID_W, NA_HEADS, NA_HEAD_DIM)
    q = rmsnorm(q.reshape(shp), q_gain) * (NA_HEAD_DIM ** -0.5)
    k = rmsnorm(k.reshape(shp), k_gain)
    v = v.reshape(shp)
    cols = np.arange(GRID_W)
    col_start = np.clip(cols - WIN_W // 2, 0, GRID_W - WIN_W)
    col_idx = col_start[:, None] + np.arange(WIN_W)[None, :]
    dc_idx = col_idx - cols[:, None] + (WIN_W - 1)

    def row_block(r):
        rs = jnp.clip(r - kh // 2, 0, rows - kh)
        k_rows = lax.dynamic_slice_in_dim(k, rs, kh, axis=1)
        v_rows = lax.dynamic_slice_in_dim(v, rs, kh, axis=1)
        k_win = k_rows[:, :, col_idx]
        v_win = v_rows[:, :, col_idx]
        q_r = lax.dynamic_index_in_dim(q, r, axis=1, keepdims=False)
        s = jnp.einsum('bchd,bicjhd->bhcij', q_r, k_win).astype(jnp.float32)
        dr_idx = rs + jnp.arange(kh) - r + (WIN_H - 1)
        bias = rpb[:, dr_idx[:, None, None], dc_idx[None, :, :]]
        s = s + jnp.transpose(bias, (0, 2, 1, 3))[None].astype(jnp.float32)
        p = jax.nn.softmax(s.reshape(B, NA_HEADS, GRID_W, kh * WIN_W), axis=-1)
        p = p.reshape(B, NA_HEADS, GRID_W, kh, WIN_W).astype(v.dtype)
        return jnp.einsum('bhcij,bicjhd->bchd', p, v_win)

    out = lax.map(row_block, jnp.arange(rows))
    out = jnp.transpose(out, (1, 0, 2, 3, 4)).reshape(B, T, D)
    return out @ w_o


def spatial_gating(h, w_in, v_gain, w_s, b_s, w_out):
    B, T, D = h.shape
    z = jax.nn.gelu(h @ w_in, approximate=False)
    u, v = jnp.split(z, 2, axis=-1)
    v = rmsnorm(v, v_gain)
    v = v.reshape(B, T // CHUNK, CHUNK, SGU_GROUPS, SGU_GROUP_DIM)
    v = jnp.einsum('gij,bnjgd->bnigd', w_s, v) + jnp.transpose(b_s)[None, None, :, :, None]
    return (u * v.reshape(B, T, D_SGU)) @ w_out


def multiscale_pool(h, w_grp, scale):
    B, T, D = h.shape
    hf = h.astype(jnp.float32)
    cs = jnp.concatenate([jnp.zeros((B, 1, D), jnp.float32), jnp.cumsum(hf, axis=1)], axis=1)
    t = jnp.arange(T)
    parts = []
    for g, w in enumerate(POOL_WINDOWS):
        lo = jnp.clip(t - w // 2, 0, T)
        hi = jnp.clip(t + w // 2, 0, T)
        sl = slice(g * POOL_GROUP_DIM, (g + 1) * POOL_GROUP_DIM)
        cg = cs[..., sl]
        cnt = (hi - lo).astype(jnp.float32)[None, :, None]
        parts.append((cg[:, hi] - cg[:, lo]) / cnt - hf[..., sl])
    p = jnp.concatenate(parts, axis=-1).astype(h.dtype).reshape(B, T, POOL_GROUPS, POOL_GROUP_DIM)
    y = jnp.einsum('btgc,gcd->btgd', p, w_grp).reshape(B, T, D)
    return y * scale


def _w(key, shape, fan_in):
    return jax.random.normal(key, shape, jnp.float32) * (fan_in ** -0.5)


def _gain(key, shape):
    return 1.0 + 0.02 * jax.random.normal(key, shape, jnp.float32)


def setup_inputs(seed: int = 0) -> dict:
    key = jax.random.key(seed)
    ks = jax.random.split(key, 24)
    D = D_MODEL
    return {
        "x": jax.random.normal(ks[0], (BATCH, SEQ, D), jnp.float32),
        "ffn1_norm": _gain(ks[1], (DEPTH, D)),
        "ffn1_w_gate": _w(ks[2], (DEPTH, D, D_FF), D),
        "ffn1_w_up": _w(ks[3], (DEPTH, D, D_FF), D),
        "ffn1_w_down": _w(ks[4], (DEPTH, D_FF, D), D_FF),
        "mix_norm": _gain(ks[5], (DEPTH, D)),
        "ffn2_norm": _gain(ks[6], (DEPTH, D)),
        "ffn2_w_gate": _w(ks[7], (DEPTH, D, D_FF), D),
        "ffn2_w_up": _w(ks[8], (DEPTH, D, D_FF), D),
        "ffn2_w_down": _w(ks[9], (DEPTH, D_FF, D), D_FF),
        "out_norm": _gain(ks[10], (DEPTH, D)),
        "na_w_qkv": _w(ks[11], (N_A, D, 3 * D), D),
        "na_q_gain": _gain(ks[12], (N_A, NA_HEAD_DIM)),
        "na_k_gain": _gain(ks[13], (N_A, NA_HEAD_DIM)),
        "na_rpb": 0.1 * jax.random.normal(ks[14], (N_A, NA_HEADS, 2 * WIN_H - 1, 2 * WIN_W - 1), jnp.float32),
        "na_w_o": _w(ks[15], (N_A, D, D), D),
        "sgu_w_in": _w(ks[16], (N_B, D, 2 * D_SGU), D),
        "sgu_v_gain": _gain(ks[17], (N_B, D_SGU)),
        "sgu_w_s": _w(ks[18], (N_B, SGU_GROUPS, CHUNK, CHUNK), CHUNK),
        "sgu_b_s": _gain(ks[19], (N_B, SGU_GROUPS, CHUNK)),
        "sgu_w_out": _w(ks[20], (N_B, D_SGU, D), D_SGU),
        "pool_w": _w(ks[21], (N_C, POOL_GROUPS, POOL_GROUP_DIM, POOL_GROUP_DIM), POOL_GROUP_DIM),
        "pool_scale": _gain(ks[22], (N_C, D)),
    }


def reference(x, ffn1_norm, ffn1_w_gate, ffn1_w_up, ffn1_w_down, mix_norm,
              ffn2_norm, ffn2_w_gate, ffn2_w_up, ffn2_w_down, out_norm,
              na_w_qkv, na_q_gain, na_k_gain, na_rpb, na_w_o,
              sgu_w_in, sgu_v_gain, sgu_w_s, sgu_b_s, sgu_w_out,
              pool_w, pool_scale):
    for i in range(DEPTH):
        x = x + 0.5 * swiglu(rmsnorm(x, ffn1_norm[i]), ffn1_w_gate[i], ffn1_w_up[i], ffn1_w_down[i])
        h = rmsnorm(x, mix_norm[i])
        kind, j = i % N_MIXERS, i // N_MIXERS
        if kind == 0:
            y = neighbourhood_attention(h, na_w_qkv[j], na_q_gain[j], na_k_gain[j], na_rpb[j], na_w_o[j])
        elif kind == 1:
            y = spatial_gating(h, sgu_w_in[j], sgu_v_gain[j], sgu_w_s[j], sgu_b_s[j], sgu_w_out[j])
        else:
            y = multiscale_pool(h, pool_w[j], pool_scale[j])
        x = x + y
        x = x + 0.5 * swiglu(rmsnorm(x, ffn2_norm[i]), ffn2_w_gate[i], ffn2_w_up[i], ffn2_w_down[i])
        x = rmsnorm(x, out_norm[i])
    return x
```

```python
import functools
import math

import jax
import jax.numpy as jnp
import numpy as np
from jax import lax
from jax.experimental import pallas as pl
from jax.experimental.pallas import tpu as pltpu

EPS = 1e-6
N_MIXERS = 3
NA_HEAD_DIM = 32
GRID_W = 64
WIN_H = 8
WIN_W = 16
CHUNK = 128
SGU_GROUPS = 16
POOL_WINDOWS = (2, 4, 8, 16)
POOL_HALO = 64

LANES = 128
MXU_N = 256
HEADS_PER_GROUP = MXU_N // NA_HEAD_DIM
VMEM_LIMIT = 56 * 1024 * 1024
MASK_VALUE = -1e30

BF16 = jnp.bfloat16
F32 = jnp.float32


def _cparams(*sem):
    return pltpu.CompilerParams(dimension_semantics=sem, vmem_limit_bytes=VMEM_LIMIT)


def _rms(x, g):
    return x * lax.rsqrt(jnp.mean(x * x, axis=-1, keepdims=True) + EPS) * g


def _dot(a, b):
    return jnp.dot(a, b, preferred_element_type=F32)


def _aligned(i, m):
    return i * m if isinstance(i, int) else pl.multiple_of(i * m, m)


def _pick(n, prefs):
    for p in prefs:
        if n % p == 0:
            return p
    return n


def _ffn_kernel(x_ref, g_ref, wg_ref, wu_ref, wd_ref, og_ref, mg_ref, y_ref, *rest,
                post_norm, emit_mix):
    if emit_mix:
        hm_ref, h_ref = rest
    else:
        (h_ref,) = rest
    j = pl.program_id(1)

    @pl.when(j == 0)
    def _():
        xf = x_ref[...]
        h_ref[...] = _rms(xf, g_ref[...]).astype(BF16)
        y_ref[...] = xf

    h = h_ref[...]
    gate = _dot(h, wg_ref[...])
    up = _dot(h, wu_ref[...])
    act = (gate * jax.nn.sigmoid(gate) * up * 0.5).astype(BF16)
    y_ref[...] += _dot(act, wd_ref[...])

    @pl.when(j == pl.num_programs(1) - 1)
    def _():
        y = y_ref[...]
        if post_norm:
            y = _rms(y, og_ref[...])
            y_ref[...] = y
        if emit_mix:
            hm_ref[...] = _rms(y, mg_ref[...]).astype(BF16)


def _ffn(x, gain, wg, wu, wd, out_gain, mix_gain, *, post_norm, emit_mix):
    m, d = x.shape
    f = wg.shape[1]
    tm = _pick(m, (512, 256, 128))
    tf = _pick(f, (512, 256, 128))
    row = lambda i, j: (i, 0)
    vec = lambda i, j: (0, 0)
    out_shape = [jax.ShapeDtypeStruct((m, d), F32)]
    out_specs = [pl.BlockSpec((tm, d), row)]
    if emit_mix:
        out_shape.append(jax.ShapeDtypeStruct((m, d), BF16))
        out_specs.append(pl.BlockSpec((tm, d), row))
    res = pl.pallas_call(
        functools.partial(_ffn_kernel, post_norm=post_norm, emit_mix=emit_mix),
        grid=(m // tm, f // tf),
        in_specs=[
            pl.BlockSpec((tm, d), row),
            pl.BlockSpec((1, d), vec),
            pl.BlockSpec((d, tf), lambda i, j: (0, j)),
            pl.BlockSpec((d, tf), lambda i, j: (0, j)),
            pl.BlockSpec((tf, d), lambda i, j: (j, 0)),
            pl.BlockSpec((1, d), vec),
            pl.BlockSpec((1, d), vec),
        ],
        out_specs=out_specs,
        out_shape=out_shape,
        scratch_shapes=[pltpu.VMEM((tm, d), BF16)],
        compiler_params=_cparams("parallel", "arbitrary"),
        name="ffn",
    )(x, gain.reshape(1, d), wg, wu, wd, out_gain.reshape(1, d), mix_gain.reshape(1, d))
    return res if emit_mix else res[0]


def _qkv_kernel(h_ref, w_ref, seg_ref, gain_ref, o_ref, *, n_norm_blocks):
    acc = _dot(h_ref[...], w_ref[...])
    j = pl.program_id(1)

    @pl.when(j < n_norm_blocks)
    def _():
        sq = acc * acc
        hi = sq.astype(BF16)
        lo = (sq - hi.astype(F32)).astype(BF16)
        seg = seg_ref[...]
        ss = _dot(hi, seg) + _dot(lo, seg)
        y = acc * lax.rsqrt(ss * (1.0 / NA_HEAD_DIM) + EPS) * gain_ref[...]
        o_ref[...] = y.astype(BF16)

    @pl.when(j >= n_norm_blocks)
    def _():
        o_ref[...] = acc.astype(BF16)


def _qkv(h, w, gains):
    m, d = h.shape
    n = w.shape[1]
    tm = _pick(m, (1024, 512, 256, 128))
    tn = MXU_N
    head = np.arange(tn) // NA_HEAD_DIM
    seg = jnp.asarray(head[:, None] == head[None, :], BF16)
    return pl.pallas_call(
        functools.partial(_qkv_kernel, n_norm_blocks=2 * d // tn),
        grid=(m // tm, n // tn),
        in_specs=[
            pl.BlockSpec((tm, d), lambda i, j: (i, 0)),
            pl.BlockSpec((d, tn), lambda i, j: (0, j)),
            pl.BlockSpec((tn, tn), lambda i, j: (0, 0)),
            pl.BlockSpec((1, tn), lambda i, j: (0, j)),
        ],
        out_specs=pl.BlockSpec((tm, tn), lambda i, j: (i, j)),
        out_shape=jax.ShapeDtypeStruct((m, n), BF16),
        compiler_params=_cparams("parallel", "arbitrary"),
        name="na_qkv",
    )(h, w, seg, gains)


def _attn_row(r_local, row, q_ref, k_ref, v_ref, bias, o_ref, head_mask, *, rows):
    kh = min(WIN_H, rows)
    rs = min(max(row - kh // 2, 0), rows - kh) if isinstance(row, int) else jnp.clip(row - kh // 2, 0, rows - kh)
    q = q_ref[0, pl.ds(_aligned(r_local, GRID_W), GRID_W), :]
    kwin = k_ref[0, pl.ds(_aligned(rs, GRID_W), kh * GRID_W), :]
    vwin = v_ref[0, pl.ds(_aligned(rs, GRID_W), kh * GRID_W), :]
    qs = jnp.where(head_mask, jnp.tile(q, (HEADS_PER_GROUP, 1)), jnp.zeros((), BF16))
    s = lax.dot_general(qs, kwin, (((1,), (1,)), ((), ())), preferred_element_type=F32)
    s = s + bias
    m = jnp.max(s, axis=-1, keepdims=True)
    p = jnp.exp(s - m)
    l = jnp.sum(p, axis=-1, keepdims=True)
    o = _dot(p.astype(BF16), vwin) * (1.0 / l)
    o = jnp.where(head_mask, o, 0.0)
    out = o[0:GRID_W]
    for hh in range(1, HEADS_PER_GROUP):
        out = out + o[hh * GRID_W:(hh + 1) * GRID_W]
    o_ref[0, pl.ds(_aligned(r_local, GRID_W), GRID_W), :] = out.astype(BF16)


def _attn_kernel(q_ref, k_ref, v_ref, t_ref, o_ref, *, rows, rows_per_step):
    kh = min(WIN_H, rows)
    rc = pl.program_id(2)
    n_rc = pl.num_programs(2)
    row0 = rc * rows_per_step
    lane_head = lax.broadcasted_iota(jnp.int32, (HEADS_PER_GROUP * GRID_W, MXU_N), 1) // NA_HEAD_DIM
    row_head = lax.broadcasted_iota(jnp.int32, (HEADS_PER_GROUP * GRID_W, MXU_N), 0) // GRID_W
    head_mask = lane_head == row_head

    def bias_for(d):
        off = (WIN_H - d) * GRID_W
        b = t_ref[:, :, off:off + kh * GRID_W]
        return b.reshape(HEADS_PER_GROUP * GRID_W, kh * GRID_W)

    body = functools.partial(_attn_row, q_ref=q_ref, k_ref=k_ref, v_ref=v_ref, o_ref=o_ref,
                             head_mask=head_mask, rows=rows)
    top = kh // 2
    bot = rows - kh + kh // 2

    lo = jnp.maximum(row0, top) - row0
    hi = jnp.minimum(row0 + rows_per_step, bot + 1) - row0

    def interior(r_local, carry):
        body(r_local, row0 + r_local, bias=bias_for(kh // 2))
        return carry

    lax.fori_loop(lo, hi, interior, 0)

    @pl.when(rc == 0)
    def _():
        for r in range(top):
            body(r, r, bias=bias_for(r))

    @pl.when(rc == n_rc - 1)
    def _():
        for r in range(bot + 1, rows):
            body(r - (rows - rows_per_step), r, bias=bias_for(r - (rows - kh)))


def _bias_table(rpb):
    n_dr = 2 * WIN_H - 1
    cols = np.arange(GRID_W)
    start = np.clip(cols - WIN_W // 2, 0, GRID_W - WIN_W)
    kc = np.arange(GRID_W)
    inside = (kc[None, :] >= start[:, None]) & (kc[None, :] < start[:, None] + WIN_W)
    dc = np.clip(kc[None, :] - cols[:, None] + WIN_W - 1, 0, 2 * WIN_W - 2)
    t = rpb[:, :, dc]
    t = jnp.where(inside[None, None], t, MASK_VALUE)
    t = jnp.transpose(t, (0, 2, 1, 3)).reshape(rpb.shape[0], GRID_W, n_dr * GRID_W)
    width = -(-(GRID_W + n_dr * GRID_W) // LANES) * LANES
    return jnp.pad(t, ((0, 0), (0, 0), (GRID_W, width - GRID_W - n_dr * GRID_W)),
                   constant_values=MASK_VALUE)


def _attention(qkv, table, batch, seq, d):
    rows = seq // GRID_W
    rps = _pick(rows, (32, 16, 8))
    qkv3 = qkv.reshape(batch, seq, 3 * d)
    n_hg = d // MXU_N
    tw = table.shape[-1]
    out = pl.pallas_call(
        functools.partial(_attn_kernel, rows=rows, rows_per_step=rps),
        grid=(batch, n_hg, rows // rps),
        in_specs=[
            pl.BlockSpec((1, rps * GRID_W, MXU_N), lambda b, g, r: (b, r, g)),
            pl.BlockSpec((1, seq, MXU_N), lambda b, g, r: (b, 0, n_hg + g)),
            pl.BlockSpec((1, seq, MXU_N), lambda b, g, r: (b, 0, 2 * n_hg + g)),
            pl.BlockSpec((HEADS_PER_GROUP, GRID_W, tw), lambda b, g, r: (g, 0, 0)),
        ],
        out_specs=pl.BlockSpec((1, rps * GRID_W, MXU_N), lambda b, g, r: (b, r, g)),
        out_shape=jax.ShapeDtypeStruct((batch, seq, d), BF16),
        compiler_params=_cparams("parallel", "parallel", "arbitrary"),
        name="na_attn",
    )(qkv3, qkv3, qkv3, table)
    return out.reshape(batch * seq, d)


def _proj_kernel(a_ref, w_ref, x_ref, o_ref):
    o_ref[...] = x_ref[...] + _dot(a_ref[...], w_ref[...])


def _proj_residual(a, w, x):
    m, k = a.shape
    n = w.shape[1]
    tm = _pick(m, (1024, 512, 256, 128))
    tn = _pick(n, (512, 256, 128))
    return pl.pallas_call(
        _proj_kernel,
        grid=(m // tm, n // tn),
        in_specs=[
            pl.BlockSpec((tm, k), lambda i, j: (i, 0)),
            pl.BlockSpec((k, tn), lambda i, j: (0, j)),
            pl.BlockSpec((tm, tn), lambda i, j: (i, j)),
        ],
        out_specs=pl.BlockSpec((tm, tn), lambda i, j: (i, j)),
        out_shape=jax.ShapeDtypeStruct((m, n), F32),
        compiler_params=_cparams("parallel", "arbitrary"),
        name="proj_residual",
    )(a, w, x)


def _na_mixer(x1, hmix, w_qkv, q_gain, k_gain, rpb, w_o, batch, seq):
    d = x1.shape[1]
    n_heads = d // NA_HEAD_DIM
    gains = jnp.concatenate([jnp.tile(q_gain * (NA_HEAD_DIM ** -0.5), n_heads),
                             jnp.tile(k_gain, n_heads),
                             jnp.ones((d,), F32)]).reshape(1, 3 * d)
    qkv = _qkv(hmix, w_qkv, gains)
    att = _attention(qkv, _bias_table(rpb), batch, seq, d)
    return _proj_residual(att, w_o, x1)


def _gelu_exact(x):
    return 0.5 * x * (1.0 + lax.erf(x * (1.0 / math.sqrt(2.0))))


def _sgu_in_kernel(h_ref, w_ref, z_ref, ss_ref, *, n_u_blocks):
    j = pl.program_id(1)
    z = _gelu_exact(_dot(h_ref[...], w_ref[...]))
    z_ref[...] = z.astype(BF16)

    @pl.when(j == 0)
    def _():
        ss_ref[...] = jnp.zeros_like(ss_ref)

    @pl.when(j >= n_u_blocks)
    def _():
        ss_ref[...] += jnp.sum(z * z, axis=-1, keepdims=True)


def _sgu_in(h, w):
    m, d = h.shape
    n = w.shape[1]
    tm = _pick(m, (1024, 512, 256, 128))
    tn = _pick(n // 2, (512, 256, 128))
    return pl.pallas_call(
        functools.partial(_sgu_in_kernel, n_u_blocks=n // 2 // tn),
        grid=(m // tm, n // tn),
        in_specs=[
            pl.BlockSpec((tm, d), lambda i, j: (i, 0)),
            pl.BlockSpec((d, tn), lambda i, j: (0, j)),
        ],
        out_specs=[
            pl.BlockSpec((tm, tn), lambda i, j: (i, j)),
            pl.BlockSpec((tm, LANES), lambda i, j: (i, 0)),
        ],
        out_shape=[jax.ShapeDtypeStruct((m, n), BF16), jax.ShapeDtypeStruct((m, LANES), F32)],
        compiler_params=_cparams("parallel", "arbitrary"),
        name="sgu_in",
    )(h, w)


def _sgu_out_kernel(u_ref, v_ref, ss_ref, vg_ref, ws_ref, bs_ref, wo_ref, x_ref, o_ref, t_ref,
                    *, d_sgu):
    tm = u_ref.shape[0]
    gd = d_sgu // SGU_GROUPS
    rinv = lax.rsqrt(ss_ref[:, 0:1] * (1.0 / d_sgu) + EPS)
    for g in range(SGU_GROUPS):
        sl = slice(g * gd, (g + 1) * gd)
        vn = (v_ref[:, sl].astype(F32) * rinv * vg_ref[:, sl]).astype(BF16)
        ws = ws_ref[g]
        bs = bs_ref[g]
        for c in range(tm // CHUNK):
            rs = slice(c * CHUNK, (c + 1) * CHUNK)
            mixed = _dot(ws, vn[rs]) + bs
            t_ref[rs, sl] = (u_ref[rs, sl].astype(F32) * mixed).astype(BF16)
    o_ref[...] = x_ref[...] + _dot(t_ref[...], wo_ref[...])


def _sgu_out(z, ss, v_gain, w_s, b_s, w_out, x):
    m, d = x.shape
    d_sgu = z.shape[1] // 2
    tm = _pick(m, (256, 128))
    const2 = lambda i: (0, 0)
    const3 = lambda i: (0, 0, 0)
    return pl.pallas_call(
        functools.partial(_sgu_out_kernel, d_sgu=d_sgu),
        grid=(m // tm,),
        in_specs=[
            pl.BlockSpec((tm, d_sgu), lambda i: (i, 0)),
            pl.BlockSpec((tm, d_sgu), lambda i: (i, 1)),
            pl.BlockSpec((tm, LANES), lambda i: (i, 0)),
            pl.BlockSpec((1, d_sgu), const2),
            pl.BlockSpec((SGU_GROUPS, CHUNK, CHUNK), const3),
            pl.BlockSpec((SGU_GROUPS, CHUNK, 1), const3),
            pl.BlockSpec((d_sgu, d), const2, pipeline_mode=pl.Buffered(1)),
            pl.BlockSpec((tm, d), lambda i: (i, 0)),
        ],
        out_specs=pl.BlockSpec((tm, d), lambda i: (i, 0)),
        out_shape=jax.ShapeDtypeStruct((m, d), F32),
        scratch_shapes=[pltpu.VMEM((tm, d_sgu), BF16)],
        compiler_params=_cparams("parallel"),
        name="sgu_out",
    )(z, z, ss, v_gain.reshape(1, d_sgu), w_s, b_s.reshape(SGU_GROUPS, CHUNK, 1), w_out, x)


def _sgu_mixer(x1, hmix, w_in, v_gain, w_s, b_s, w_out):
    z, ss = _sgu_in(hmix, w_in)
    return _sgu_out(z, ss, v_gain, w_s, b_s, w_out, x1)


def _pool_kernel(xp_ref, xc_ref, xn_ref, g_ref, w_ref, sc_ref, o_ref, *, seq):
    tm, d = xc_ref.shape
    gd = d // len(POOL_WINDOWS)
    gain = g_ref[...]
    xc = xc_ref[...]
    hc = _rms(xc, gain)
    hext = jnp.concatenate([_rms(xp_ref[...], gain), hc, _rms(xn_ref[...], gain)], axis=0)
    hext = hext.astype(BF16)
    pos0 = (pl.program_id(0) * tm) % seq
    t = lax.broadcasted_iota(jnp.int32, (tm, tm + 2 * POOL_HALO), 0)
    off = lax.broadcasted_iota(jnp.int32, (tm, tm + 2 * POOL_HALO), 1) - POOL_HALO - t
    key_pos = pos0 + t + off
    in_seq = (key_pos >= 0) & (key_pos < seq)
    tq = pos0 + lax.broadcasted_iota(jnp.int32, (tm, 1), 0)
    for g, w in enumerate(POOL_WINDOWS):
        sl = slice(g * gd, (g + 1) * gd)
        band = ((off >= -(w // 2)) & (off < w // 2) & in_seq).astype(BF16)
        cnt = (jnp.clip(tq + w // 2, 0, seq) - jnp.clip(tq - w // 2, 0, seq)).astype(F32)
        p = _dot(band, hext[:, sl]) / cnt - hc[:, sl]
        y = _dot(p.astype(BF16), w_ref[g])
        o_ref[:, sl] = xc[:, sl] + y * sc_ref[:, sl]


def _pool_mixer(x, gain, w_grp, scale, seq):
    m, d = x.shape
    tm = _pick(seq, (512, 256, 128))
    nb = tm // POOL_HALO
    last = m // POOL_HALO - 1
    ng, gd, _ = w_grp.shape
    return pl.pallas_call(
        functools.partial(_pool_kernel, seq=seq),
        grid=(m // tm,),
        in_specs=[
            pl.BlockSpec((POOL_HALO, d), lambda i: (jnp.maximum(i * nb - 1, 0), 0)),
            pl.BlockSpec((tm, d), lambda i: (i, 0)),
            pl.BlockSpec((POOL_HALO, d), lambda i: (jnp.minimum((i + 1) * nb, last), 0)),
            pl.BlockSpec((1, d), lambda i: (0, 0)),
            pl.BlockSpec((ng, gd, gd), lambda i: (0, 0, 0)),
            pl.BlockSpec((1, d), lambda i: (0, 0)),
        ],
        out_specs=pl.BlockSpec((tm, d), lambda i: (i, 0)),
        out_shape=jax.ShapeDtypeStruct((m, d), F32),
        compiler_params=_cparams("parallel"),
        name="pool_mixer",
    )(x, x, x, gain.reshape(1, d), w_grp, scale.reshape(1, d))


def kernel(x, ffn1_norm, ffn1_w_gate, ffn1_w_up, ffn1_w_down, mix_norm, ffn2_norm, ffn2_w_gate, ffn2_w_up, ffn2_w_down, out_norm, na_w_qkv, na_q_gain, na_k_gain, na_rpb, na_w_o, sgu_w_in, sgu_v_gain, sgu_w_s, sgu_b_s, sgu_w_out, pool_w, pool_scale):
    batch, seq, d = x.shape
    depth = ffn1_norm.shape[0]
    bf = lambda w: w.astype(BF16)
    xs = x.reshape(batch * seq, d)
    for i in range(depth):
        kind, j = i % N_MIXERS, i // N_MIXERS
        emit_mix = kind != 2
        res = _ffn(xs, ffn1_norm[i], bf(ffn1_w_gate[i]), bf(ffn1_w_up[i]), bf(ffn1_w_down[i]),
                   out_norm[i], mix_norm[i], post_norm=False, emit_mix=emit_mix)
        if kind == 0:
            x1, hmix = res
            x2 = _na_mixer(x1, hmix, bf(na_w_qkv[j]), na_q_gain[j], na_k_gain[j], na_rpb[j],
                           bf(na_w_o[j]), batch, seq)
        elif kind == 1:
            x1, hmix = res
            x2 = _sgu_mixer(x1, hmix, bf(sgu_w_in[j]), sgu_v_gain[j], bf(sgu_w_s[j]), sgu_b_s[j],
                            bf(sgu_w_out[j]))
        else:
            x2 = _pool_mixer(res, mix_norm[i], bf(pool_w[j]), pool_scale[j], seq)
        xs = _ffn(x2, ffn2_norm[i], bf(ffn2_w_gate[i]), bf(ffn2_w_up[i]), bf(ffn2_w_down[i]),
                  out_norm[i], mix_norm[i], post_norm=True, emit_mix=False)
    return xs.reshape(batch, seq, d)
```

```python
import functools
import math

import jax
import jax.numpy as jnp
import numpy as np
from jax import lax
from jax.experimental import pallas as pl
from jax.experimental.pallas import tpu as pltpu

EPS = 1e-6
N_MIXERS = 3
NA_HEAD_DIM = 32
GRID_W = 64
WIN_H = 8
WIN_W = 16
CHUNK = 128
SGU_GROUPS = 16
POOL_WINDOWS = (2, 4, 8, 16)
POOL_HALO = 64

LANES = 128
MXU_N = 256
HEADS_PER_GROUP = MXU_N // NA_HEAD_DIM
VMEM_LIMIT = 60000 * 1024
CAST_BLOCK_BYTES = 6 * 1024 * 1024
MASK_VALUE = -1e30

BF16 = jnp.bfloat16
F32 = jnp.float32


def _cparams(*sem):
    return pltpu.CompilerParams(dimension_semantics=sem, vmem_limit_bytes=VMEM_LIMIT)


def _rms(x, g):
    return x * lax.rsqrt(jnp.mean(x * x, axis=-1, keepdims=True) + EPS) * g


def _dot(a, b):
    return jnp.dot(a, b, preferred_element_type=F32)


def _aligned(i, m):
    return i * m if isinstance(i, int) else pl.multiple_of(i * m, m)


def _pick(n, prefs):
    for p in prefs:
        if n % p == 0:
            return p
    return n


def _cast_kernel(w_ref, o_ref):
    o_ref[...] = w_ref[...].astype(BF16)


def _cast_bf16(w):
    l, r, c = w.shape
    tr = r
    while tr * c * 4 > CAST_BLOCK_BYTES and tr % 2 == 0 and (tr // 2) % 16 == 0:
        tr //= 2
    return pl.pallas_call(
        _cast_kernel,
        grid=(l, r // tr),
        in_specs=[pl.BlockSpec((1, tr, c), lambda a, b: (a, b, 0))],
        out_specs=pl.BlockSpec((1, tr, c), lambda a, b: (a, b, 0)),
        out_shape=jax.ShapeDtypeStruct(w.shape, BF16),
        compiler_params=_cparams("parallel", "parallel"),
        name="cast_bf16",
    )(w)


def _ffn_kernel(x_ref, g_ref, wg_ref, wu_ref, wd_ref, og_ref, mg_ref, y_ref, *rest,
                post_norm, emit_mix):
    if emit_mix:
        (hm_ref,) = rest
        h_ref = hm_ref
    else:
        (h_ref,) = rest
    j = pl.program_id(1)

    @pl.when(j == 0)
    def _():
        xf = x_ref[...]
        h_ref[...] = _rms(xf, g_ref[...]).astype(BF16)
        y_ref[...] = xf

    h = h_ref[...]
    gate = _dot(h, wg_ref[...])
    up = _dot(h, wu_ref[...])
    act = (gate * jax.nn.sigmoid(gate) * up * 0.5).astype(BF16)
    y_ref[...] += _dot(act, wd_ref[...])

    @pl.when(j == pl.num_programs(1) - 1)
    def _():
        y = y_ref[...]
        if post_norm:
            y = _rms(y, og_ref[...])
            y_ref[...] = y
        if emit_mix:
            hm_ref[...] = _rms(y, mg_ref[...]).astype(BF16)


def _ffn(x, gain, wg, wu, wd, layer, out_gain, mix_gain, *, post_norm, emit_mix):
    m, d = x.shape
    f = wg.shape[2]
    tm = _pick(m, (1024, 512, 256, 128))
    tf = _pick(f, (512, 256, 128))
    row = lambda i, j: (i, 0)
    vec = lambda i, j: (0, 0)
    out_shape = [jax.ShapeDtypeStruct((m, d), F32)]
    out_specs = [pl.BlockSpec((tm, d), row)]
    if emit_mix:
        out_shape.append(jax.ShapeDtypeStruct((m, d), BF16))
        out_specs.append(pl.BlockSpec((tm, d), row))
    res = pl.pallas_call(
        functools.partial(_ffn_kernel, post_norm=post_norm, emit_mix=emit_mix),
        grid=(m // tm, f // tf),
        in_specs=[
            pl.BlockSpec((tm, d), row),
            pl.BlockSpec((1, d), vec),
            pl.BlockSpec((None, d, tf), lambda i, j: (layer, 0, j)),
            pl.BlockSpec((None, d, tf), lambda i, j: (layer, 0, j)),
            pl.BlockSpec((None, tf, d), lambda i, j: (layer, j, 0)),
            pl.BlockSpec((1, d), vec),
            pl.BlockSpec((1, d), vec),
        ],
        out_specs=out_specs,
        out_shape=out_shape,
        scratch_shapes=[] if emit_mix else [pltpu.VMEM((tm, d), BF16)],
        compiler_params=_cparams("parallel", "arbitrary"),
        name="ffn",
    )(x, gain.reshape(1, d), wg, wu, wd, out_gain.reshape(1, d), mix_gain.reshape(1, d))
    return res if emit_mix else res[0]


def _qkv_kernel(h_ref, w_ref, seg_ref, gain_ref, o_ref, *, n_norm_blocks):
    acc = _dot(h_ref[...], w_ref[...])
    j = pl.program_id(1)

    @pl.when(j < n_norm_blocks)
    def _():
        seg = seg_ref[...]
        for c in range(acc.shape[1] // MXU_N):
            sl = slice(c * MXU_N, (c + 1) * MXU_N)
            a = acc[:, sl]
            sq = a * a
            hi = sq.astype(BF16)
            lo = (sq - hi.astype(F32)).astype(BF16)
            ss = _dot(hi, seg) + _dot(lo, seg)
            y = a * lax.rsqrt(ss * (1.0 / NA_HEAD_DIM) + EPS) * gain_ref[:, sl]
            o_ref[:, sl] = y.astype(BF16)

    @pl.when(j >= n_norm_blocks)
    def _():
        o_ref[...] = acc.astype(BF16)


def _qkv(h, w, layer, gains):
    m, d = h.shape
    n = w.shape[2]
    tm = _pick(m, (1024, 512, 256, 128))
    tn = _pick(d, (1024, 512, 256))
    head = np.arange(MXU_N) // NA_HEAD_DIM
    seg = jnp.asarray(head[:, None] == head[None, :], BF16)
    return pl.pallas_call(
        functools.partial(_qkv_kernel, n_norm_blocks=2 * d // tn),
        grid=(m // tm, n // tn),
        in_specs=[
            pl.BlockSpec((tm, d), lambda i, j: (i, 0)),
            pl.BlockSpec((None, d, tn), lambda i, j: (layer, 0, j)),
            pl.BlockSpec((MXU_N, MXU_N), lambda i, j: (0, 0)),
            pl.BlockSpec((1, tn), lambda i, j: (0, j)),
        ],
        out_specs=pl.BlockSpec((tm, tn), lambda i, j: (i, j)),
        out_shape=jax.ShapeDtypeStruct((m, n), BF16),
        compiler_params=_cparams("parallel", "arbitrary"),
        name="na_qkv",
    )(h, w, seg, gains)


def _attn_row(r_local, row, q_ref, k_ref, v_ref, bias, o_ref, head_mask, *, rows):
    kh = min(WIN_H, rows)
    if isinstance(row, int):
        rs = min(max(row - kh // 2, 0), rows - kh)
    else:
        rs = jnp.clip(row - kh // 2, 0, rows - kh)
    q = q_ref[0, pl.ds(_aligned(r_local, GRID_W), GRID_W), :]
    kwin = k_ref[0, pl.ds(_aligned(rs, GRID_W), kh * GRID_W), :]
    vwin = v_ref[0, pl.ds(_aligned(rs, GRID_W), kh * GRID_W), :]
    qs = jnp.where(head_mask, jnp.tile(q, (HEADS_PER_GROUP, 1)), jnp.zeros((), BF16))
    s = lax.dot_general(qs, kwin, (((1,), (1,)), ((), ())), preferred_element_type=F32)
    s = s + bias
    m = jnp.max(s, axis=-1, keepdims=True)
    p = jnp.exp(s - m)
    l = jnp.sum(p, axis=-1, keepdims=True)
    o = _dot(p.astype(BF16), vwin) * (1.0 / l)
    o = jnp.where(head_mask, o, 0.0)
    out = o[0:GRID_W]
    for hh in range(1, HEADS_PER_GROUP):
        out = out + o[hh * GRID_W:(hh + 1) * GRID_W]
    o_ref[0, pl.ds(_aligned(r_local, GRID_W), GRID_W), :] = out.astype(BF16)


def _attn_kernel(q_ref, k_ref, v_ref, t_ref, o_ref, *, rows, rows_per_step):
    kh = min(WIN_H, rows)
    half = kh // 2
    rc = pl.program_id(2)
    n_rc = pl.num_programs(2)
    row0 = rc * rows_per_step
    lane_head = lax.broadcasted_iota(jnp.int32, (HEADS_PER_GROUP * GRID_W, MXU_N), 1) // NA_HEAD_DIM
    row_head = lax.broadcasted_iota(jnp.int32, (HEADS_PER_GROUP * GRID_W, MXU_N), 0) // GRID_W
    head_mask = lane_head == row_head

    def bias_for(d):
        off = (WIN_H - d) * GRID_W
        b = t_ref[:, :, off:off + kh * GRID_W]
        return b.reshape(HEADS_PER_GROUP * GRID_W, kh * GRID_W)

    body = functools.partial(_attn_row, q_ref=q_ref, k_ref=k_ref, v_ref=v_ref, o_ref=o_ref,
                             head_mask=head_mask, rows=rows)
    top = half
    bot = rows - kh + half

    lo = jnp.maximum(row0, top) - row0
    hi = jnp.minimum(row0 + rows_per_step, bot) - row0

    def pair(p, carry):
        for u in range(2):
            r_local = 2 * p + u
            body(r_local, row0 + r_local, bias=bias_for(half))
        return carry

    lax.fori_loop(lo // 2, hi // 2, pair, 0)

    @pl.when(rc == 0)
    def _():
        for r in range(top):
            body(r, r, bias=bias_for(r))

    @pl.when(rc == n_rc - 1)
    def _():
        for r in range(bot, rows):
            body(r - (rows - rows_per_step), r, bias=bias_for(r - (rows - kh)))


def _bias_table(rpb):
    n_dr = 2 * WIN_H - 1
    cols = np.arange(GRID_W)
    start = np.clip(cols - WIN_W // 2, 0, GRID_W - WIN_W)
    kc = np.arange(GRID_W)
    inside = (kc[None, :] >= start[:, None]) & (kc[None, :] < start[:, None] + WIN_W)
    dc = np.clip(kc[None, :] - cols[:, None] + WIN_W - 1, 0, 2 * WIN_W - 2)
    t = rpb[:, :, dc]
    t = jnp.where(inside[None, None], t, MASK_VALUE)
    t = jnp.transpose(t, (0, 2, 1, 3)).reshape(rpb.shape[0], GRID_W, n_dr * GRID_W)
    width = -(-(GRID_W + n_dr * GRID_W) // LANES) * LANES
    return jnp.pad(t, ((0, 0), (0, 0), (GRID_W, width - GRID_W - n_dr * GRID_W)),
                   constant_values=MASK_VALUE)


def _attention(qkv, table, batch, seq, d):
    rows = seq // GRID_W
    rps = _pick(rows, (32, 16, 8))
    kh = min(WIN_H, rows)
    assert rps % 2 == 0 and (kh // 2) % 2 == 0 and rows >= 2 * kh
    qkv3 = qkv.reshape(batch, seq, 3 * d)
    n_hg = d // MXU_N
    tw = table.shape[-1]
    out = pl.pallas_call(
        functools.partial(_attn_kernel, rows=rows, rows_per_step=rps),
        grid=(batch, n_hg, rows // rps),
        in_specs=[
            pl.BlockSpec((1, rps * GRID_W, MXU_N), lambda b, g, r: (b, r, g)),
            pl.BlockSpec((1, seq, MXU_N), lambda b, g, r: (b, 0, n_hg + g)),
            pl.BlockSpec((1, seq, MXU_N), lambda b, g, r: (b, 0, 2 * n_hg + g)),
            pl.BlockSpec((HEADS_PER_GROUP, GRID_W, tw), lambda b, g, r: (g, 0, 0)),
        ],
        out_specs=pl.BlockSpec((1, rps * GRID_W, MXU_N), lambda b, g, r: (b, r, g)),
        out_shape=jax.ShapeDtypeStruct((batch, seq, d), BF16),
        compiler_params=_cparams("parallel", "parallel", "arbitrary"),
        name="na_attn",
    )(qkv3, qkv3, qkv3, table)
    return out.reshape(batch * seq, d)


def _proj_kernel(a_ref, w_ref, x_ref, o_ref):
    o_ref[...] = x_ref[...] + _dot(a_ref[...], w_ref[...])


def _proj_residual(a, w, layer, x):
    m, k = a.shape
    n = w.shape[2]
    tm = _pick(m, (1024, 512, 256, 128))
    tn = _pick(n, (1024, 512, 256, 128))
    return pl.pallas_call(
        _proj_kernel,
        grid=(m // tm, n // tn),
        in_specs=[
            pl.BlockSpec((tm, k), lambda i, j: (i, 0)),
            pl.BlockSpec((None, k, tn), lambda i, j: (layer, 0, j)),
            pl.BlockSpec((tm, tn), lambda i, j: (i, j)),
        ],
        out_specs=pl.BlockSpec((tm, tn), lambda i, j: (i, j)),
        out_shape=jax.ShapeDtypeStruct((m, n), F32),
        compiler_params=_cparams("parallel", "arbitrary"),
        name="proj_residual",
    )(a, w, x)


def _na_mixer(x1, hmix, w_qkv, q_gain, k_gain, rpb, w_o, layer, batch, seq):
    d = x1.shape[1]
    n_heads = d // NA_HEAD_DIM
    gains = jnp.concatenate([jnp.tile(q_gain * (NA_HEAD_DIM ** -0.5), n_heads),
                             jnp.tile(k_gain, n_heads),
                             jnp.ones((d,), F32)]).reshape(1, 3 * d)
    qkv = _qkv(hmix, w_qkv, layer, gains)
    att = _attention(qkv, _bias_table(rpb), batch, seq, d)
    return _proj_residual(att, w_o, layer, x1)


def _gelu_exact(x):
    return 0.5 * x * (1.0 + lax.erf(x * (1.0 / math.sqrt(2.0))))


def _sgu_in_kernel(h_ref, w_ref, z_ref, ss_ref, *, n_u_blocks):
    j = pl.program_id(1)
    z = _gelu_exact(_dot(h_ref[...], w_ref[...]))
    z_ref[...] = z.astype(BF16)

    @pl.when(j == 0)
    def _():
        ss_ref[...] = jnp.zeros_like(ss_ref)

    @pl.when(j >= n_u_blocks)
    def _():
        ss_ref[...] += jnp.sum(z * z, axis=-1, keepdims=True)


def _sgu_in(h, w, layer):
    m, d = h.shape
    n = w.shape[2]
    tm = _pick(m, (1024, 512, 256, 128))
    tn = _pick(n // 2, (512, 256, 128))
    return pl.pallas_call(
        functools.partial(_sgu_in_kernel, n_u_blocks=n // 2 // tn),
        grid=(m // tm, n // tn),
        in_specs=[
            pl.BlockSpec((tm, d), lambda i, j: (i, 0)),
            pl.BlockSpec((None, d, tn), lambda i, j: (layer, 0, j)),
        ],
        out_specs=[
            pl.BlockSpec((tm, tn), lambda i, j: (i, j)),
            pl.BlockSpec((tm, LANES), lambda i, j: (i, 0)),
        ],
        out_shape=[jax.ShapeDtypeStruct((m, n), BF16), jax.ShapeDtypeStruct((m, LANES), F32)],
        compiler_params=_cparams("parallel", "arbitrary"),
        name="sgu_in",
    )(h, w)


def _sgu_out_kernel(u_ref, v_ref, ss_ref, vg_ref, ws_ref, bs_ref, wo_ref, x_ref, o_ref, t_ref,
                    *, d_sgu):
    tm = u_ref.shape[0]
    gd = d_sgu // SGU_GROUPS
    rinv = lax.rsqrt(ss_ref[:, 0:1] * (1.0 / d_sgu) + EPS)
    for g in range(SGU_GROUPS):
        sl = slice(g * gd, (g + 1) * gd)
        vn = (v_ref[:, sl].astype(F32) * rinv * vg_ref[:, sl]).astype(BF16)
        ws = ws_ref[g]
        bs = bs_ref[g]
        for c in range(tm // CHUNK):
            rs = slice(c * CHUNK, (c + 1) * CHUNK)
            mixed = _dot(ws, vn[rs]) + bs
            t_ref[rs, sl] = (u_ref[rs, sl].astype(F32) * mixed).astype(BF16)
    o_ref[...] = x_ref[...] + _dot(t_ref[...], wo_ref[...])


def _sgu_out(z, ss, v_gain, w_s, b_s, w_out, layer, x):
    m, d = x.shape
    d_sgu = z.shape[1] // 2
    tm = _pick(m, (256, 128))
    const2 = lambda i: (0, 0)
    const3 = lambda i: (0, 0, 0)
    return pl.pallas_call(
        functools.partial(_sgu_out_kernel, d_sgu=d_sgu),
        grid=(m // tm,),
        in_specs=[
            pl.BlockSpec((tm, d_sgu), lambda i: (i, 0)),
            pl.BlockSpec((tm, d_sgu), lambda i: (i, 1)),
            pl.BlockSpec((tm, LANES), lambda i: (i, 0)),
            pl.BlockSpec((1, d_sgu), const2),
            pl.BlockSpec((SGU_GROUPS, CHUNK, CHUNK), const3),
            pl.BlockSpec((SGU_GROUPS, CHUNK, 1), const3),
            pl.BlockSpec((None, d_sgu, d), lambda i: (layer, 0, 0), pipeline_mode=pl.Buffered(1)),
            pl.BlockSpec((tm, d), lambda i: (i, 0)),
        ],
        out_specs=pl.BlockSpec((tm, d), lambda i: (i, 0)),
        out_shape=jax.ShapeDtypeStruct((m, d), F32),
        scratch_shapes=[pltpu.VMEM((tm, d_sgu), BF16)],
        compiler_params=_cparams("parallel"),
        name="sgu_out",
    )(z, z, ss, v_gain.reshape(1, d_sgu), w_s, b_s.reshape(SGU_GROUPS, CHUNK, 1), w_out, x)


def _sgu_mixer(x1, hmix, w_in, v_gain, w_s, b_s, w_out, layer):
    z, ss = _sgu_in(hmix, w_in, layer)
    return _sgu_out(z, ss, v_gain, w_s, b_s, w_out, layer, x1)


def _pool_kernel(xp_ref, xc_ref, xn_ref, g_ref, w_ref, sc_ref, o_ref, *, seq):
    tm, d = xc_ref.shape
    gd = d // len(POOL_WINDOWS)
    gain = g_ref[...]
    xc = xc_ref[...]
    hc = _rms(xc, gain)
    hext = jnp.concatenate([_rms(xp_ref[...], gain), hc, _rms(xn_ref[...], gain)], axis=0)
    hext = hext.astype(BF16)
    pos0 = (pl.program_id(0) * tm) % seq
    t = lax.broadcasted_iota(jnp.int32, (tm, tm + 2 * POOL_HALO), 0)
    off = lax.broadcasted_iota(jnp.int32, (tm, tm + 2 * POOL_HALO), 1) - POOL_HALO - t
    key_pos = pos0 + t + off
    in_seq = (key_pos >= 0) & (key_pos < seq)
    tq = pos0 + lax.broadcasted_iota(jnp.int32, (tm, 1), 0)
    for g, w in enumerate(POOL_WINDOWS):
        sl = slice(g * gd, (g + 1) * gd)
        band = ((off >= -(w // 2)) & (off < w // 2) & in_seq).astype(BF16)
        cnt = (jnp.clip(tq + w // 2, 0, seq) - jnp.clip(tq - w // 2, 0, seq)).astype(F32)
        p = _dot(band, hext[:, sl]) / cnt - hc[:, sl]
        y = _dot(p.astype(BF16), w_ref[g])
        o_ref[:, sl] = xc[:, sl] + y * sc_ref[:, sl]


def _pool_mixer(x, gain, w_grp, scale, seq):
    m, d = x.shape
    tm = _pick(seq, (512, 256, 128))
    nb = tm // POOL_HALO
    last = m // POOL_HALO - 1
    ng, gd, _ = w_grp.shape
    return pl.pallas_call(
        functools.partial(_pool_kernel, seq=seq),
        grid=(m // tm,),
        in_specs=[
            pl.BlockSpec((POOL_HALO, d), lambda i: (jnp.maximum(i * nb - 1, 0), 0)),
            pl.BlockSpec((tm, d), lambda i: (i, 0)),
            pl.BlockSpec((POOL_HALO, d), lambda i: (jnp.minimum((i + 1) * nb, last), 0)),
            pl.BlockSpec((1, d), lambda i: (0, 0)),
            pl.BlockSpec((ng, gd, gd), lambda i: (0, 0, 0)),
            pl.BlockSpec((1, d), lambda i: (0, 0)),
        ],
        out_specs=pl.BlockSpec((tm, d), lambda i: (i, 0)),
        out_shape=jax.ShapeDtypeStruct((m, d), F32),
        compiler_params=_cparams("parallel"),
        name="pool_mixer",
    )(x, x, x, gain.reshape(1, d), w_grp, scale.reshape(1, d))


def kernel(x, ffn1_norm, ffn1_w_gate, ffn1_w_up, ffn1_w_down, mix_norm, ffn2_norm, ffn2_w_gate, ffn2_w_up, ffn2_w_down, out_norm, na_w_qkv, na_q_gain, na_k_gain, na_rpb, na_w_o, sgu_w_in, sgu_v_gain, sgu_w_s, sgu_b_s, sgu_w_out, pool_w, pool_scale):
    batch, seq, d = x.shape
    depth = ffn1_norm.shape[0]
    f1g, f1u, f1d = _cast_bf16(ffn1_w_gate), _cast_bf16(ffn1_w_up), _cast_bf16(ffn1_w_down)
    f2g, f2u, f2d = _cast_bf16(ffn2_w_gate), _cast_bf16(ffn2_w_up), _cast_bf16(ffn2_w_down)
    w_qkv, w_o = _cast_bf16(na_w_qkv), _cast_bf16(na_w_o)
    w_in, w_out = _cast_bf16(sgu_w_in), _cast_bf16(sgu_w_out)
    xs = x.reshape(batch * seq, d)
    for i in range(depth):
        kind, j = i % N_MIXERS, i // N_MIXERS
        emit_mix = kind != 2
        res = _ffn(xs, ffn1_norm[i], f1g, f1u, f1d, i, out_norm[i], mix_norm[i],
                   post_norm=False, emit_mix=emit_mix)
        if kind == 0:
            x1, hmix = res
            x2 = _na_mixer(x1, hmix, w_qkv, na_q_gain[j], na_k_gain[j], na_rpb[j], w_o, j, batch, seq)
        elif kind == 1:
            x1, hmix = res
            x2 = _sgu_mixer(x1, hmix, w_in, sgu_v_gain[j], sgu_w_s[j].astype(BF16), sgu_b_s[j],
                            w_out, j)
        else:
            x2 = _pool_mixer(res, mix_norm[i], pool_w[j].astype(BF16), pool_scale[j], seq)
        xs = _ffn(x2, ffn2_norm[i], f2g, f2u, f2d, i, out_norm[i], mix_norm[i],
                  post_norm=True, emit_mix=False)
    return xs.reshape(batch, seq, d)
```

```python
import functools
import math

import jax
import jax.numpy as jnp
import numpy as np
from jax import lax
from jax.experimental import pallas as pl
from jax.experimental.pallas import tpu as pltpu

EPS = 1e-6
N_MIXERS = 3
NA_HEAD_DIM = 32
GRID_W = 64
WIN_H = 8
WIN_W = 16
CHUNK = 128
SGU_GROUPS = 16
POOL_WINDOWS = (2, 4, 8, 16)
POOL_HALO = 64

LANES = 128
MXU_N = 256
HEADS_PER_GROUP = MXU_N // NA_HEAD_DIM
VMEM_LIMIT = 60000 * 1024
CAST_BLOCK_BYTES = 6 * 1024 * 1024
MASK_VALUE = -1e30

BF16 = jnp.bfloat16
F32 = jnp.float32


def _cparams(*sem):
    return pltpu.CompilerParams(dimension_semantics=sem, vmem_limit_bytes=VMEM_LIMIT)


def _rms(x, g):
    return x * lax.rsqrt(jnp.mean(x * x, axis=-1, keepdims=True) + EPS) * g


def _dot(a, b):
    return jnp.dot(a, b, preferred_element_type=F32)


def _aligned(i, m):
    return i * m if isinstance(i, int) else pl.multiple_of(i * m, m)


def _pick(n, prefs):
    for p in prefs:
        if n % p == 0:
            return p
    return n


def _cast_kernel(w_ref, o_ref):
    o_ref[...] = w_ref[...].astype(BF16)


def _cast_bf16(w):
    l, r, c = w.shape
    if l == 0:
        return w.astype(BF16)
    tr = r
    while tr * c * 4 > CAST_BLOCK_BYTES and tr % 2 == 0 and (tr // 2) % 16 == 0:
        tr //= 2
    return pl.pallas_call(
        _cast_kernel,
        grid=(l, r // tr),
        in_specs=[pl.BlockSpec((1, tr, c), lambda a, b: (a, b, 0))],
        out_specs=pl.BlockSpec((1, tr, c), lambda a, b: (a, b, 0)),
        out_shape=jax.ShapeDtypeStruct(w.shape, BF16),
        compiler_params=_cparams("parallel", "parallel"),
        name="cast_bf16",
    )(w)


def _ffn_kernel(x_ref, g_ref, wg_ref, wu_ref, wd_ref, og_ref, mg_ref, y_ref, *rest,
                post_norm, emit_mix):
    if emit_mix:
        (hm_ref,) = rest
        h_ref = hm_ref
    else:
        (h_ref,) = rest
    j = pl.program_id(1)

    @pl.when(j == 0)
    def _():
        xf = x_ref[...]
        h_ref[...] = _rms(xf, g_ref[...]).astype(BF16)
        y_ref[...] = xf

    h = h_ref[...]
    gate = _dot(h, wg_ref[...])
    up = _dot(h, wu_ref[...])
    act = (gate * jax.nn.sigmoid(gate) * up * 0.5).astype(BF16)
    y_ref[...] += _dot(act, wd_ref[...])

    @pl.when(j == pl.num_programs(1) - 1)
    def _():
        y = y_ref[...]
        if post_norm:
            y = _rms(y, og_ref[...])
            y_ref[...] = y
        if emit_mix:
            hm_ref[...] = _rms(y, mg_ref[...]).astype(BF16)


def _ffn(x, gain, wg, wu, wd, layer, out_gain, mix_gain, *, post_norm, emit_mix):
    m, d = x.shape
    f = wg.shape[2]
    tm = _pick(m, (1024, 512, 256, 128))
    tf = _pick(f, (512, 256, 128))
    row = lambda i, j: (i, 0)
    vec = lambda i, j: (0, 0)
    out_shape = [jax.ShapeDtypeStruct((m, d), F32)]
    out_specs = [pl.BlockSpec((tm, d), row)]
    if emit_mix:
        out_shape.append(jax.ShapeDtypeStruct((m, d), BF16))
        out_specs.append(pl.BlockSpec((tm, d), row))
    res = pl.pallas_call(
        functools.partial(_ffn_kernel, post_norm=post_norm, emit_mix=emit_mix),
        grid=(m // tm, f // tf),
        in_specs=[
            pl.BlockSpec((tm, d), row),
            pl.BlockSpec((1, d), vec),
            pl.BlockSpec((None, d, tf), lambda i, j: (layer, 0, j)),
            pl.BlockSpec((None, d, tf), lambda i, j: (layer, 0, j)),
            pl.BlockSpec((None, tf, d), lambda i, j: (layer, j, 0)),
            pl.BlockSpec((1, d), vec),
            pl.BlockSpec((1, d), vec),
        ],
        out_specs=out_specs,
        out_shape=out_shape,
        scratch_shapes=[] if emit_mix else [pltpu.VMEM((tm, d), BF16)],
        compiler_params=_cparams("parallel", "arbitrary"),
        name="ffn",
    )(x, gain.reshape(1, d), wg, wu, wd, out_gain.reshape(1, d), mix_gain.reshape(1, d))
    return res if emit_mix else res[0]


def _qkv_kernel(h_ref, w_ref, seg_ref, gain_ref, o_ref, *, n_norm_blocks):
    j = pl.program_id(1)

    @pl.when(j < n_norm_blocks)
    def _():
        seg = seg_ref[...]
        h = h_ref[...]
        n_sub = w_ref.shape[1] // MXU_N
        cols = [slice(c * MXU_N, (c + 1) * MXU_N) for c in range(n_sub)]

        def normalise(a, sl):
            sq = a * a
            hi = sq.astype(BF16)
            lo = (sq - hi.astype(F32)).astype(BF16)
            ss = _dot(hi, seg) + _dot(lo, seg)
            y = a * lax.rsqrt(ss * (1.0 / NA_HEAD_DIM) + EPS) * gain_ref[:, sl]
            o_ref[:, sl] = y.astype(BF16)

        a_prev = _dot(h, w_ref[:, cols[0]])
        for c in range(1, n_sub):
            a_next = _dot(h, w_ref[:, cols[c]])
            normalise(a_prev, cols[c - 1])
            a_prev = a_next
        normalise(a_prev, cols[n_sub - 1])

    @pl.when(j >= n_norm_blocks)
    def _():
        o_ref[...] = _dot(h_ref[...], w_ref[...]).astype(BF16)


def _qkv(h, w, layer, gains):
    m, d = h.shape
    n = w.shape[2]
    tm = _pick(m, (1024, 512, 256, 128))
    tn = _pick(d, (1024, 512, 256))
    head = np.arange(MXU_N) // NA_HEAD_DIM
    seg = jnp.asarray(head[:, None] == head[None, :], BF16)
    return pl.pallas_call(
        functools.partial(_qkv_kernel, n_norm_blocks=2 * d // tn),
        grid=(m // tm, n // tn),
        in_specs=[
            pl.BlockSpec((tm, d), lambda i, j: (i, 0)),
            pl.BlockSpec((None, d, tn), lambda i, j: (layer, 0, j)),
            pl.BlockSpec((MXU_N, MXU_N), lambda i, j: (0, 0)),
            pl.BlockSpec((1, tn), lambda i, j: (0, j)),
        ],
        out_specs=pl.BlockSpec((tm, tn), lambda i, j: (i, j)),
        out_shape=jax.ShapeDtypeStruct((m, n), BF16),
        compiler_params=_cparams("parallel", "arbitrary"),
        name="na_qkv",
    )(h, w, seg, gains)


KEY_BLK = 32
QUERY_BLOCKS = ((0, 24, 0), (24, 16, 16), (40, 24, 32))
ATTN_ROWS = HEADS_PER_GROUP * GRID_W
ROWS_PER_ITER = 4


def _check_query_blocks():
    cols = np.arange(GRID_W)
    start = np.clip(cols - WIN_W // 2, 0, GRID_W - WIN_W)
    covered = 0
    for c0q, nq, c0k in QUERY_BLOCKS:
        assert c0q == covered and c0k % 16 == 0 and nq % 8 == 0
        assert (start[c0q:c0q + nq] >= c0k).all() and (start[c0q:c0q + nq] + WIN_W <= c0k + KEY_BLK).all()
        covered += nq
    assert covered == GRID_W


_check_query_blocks()


def _attn_rows(items, q_ref, k_ref, v_ref, t_ref, qm_ref, o_ref, *, rows):
    nt = (((1,), (1,)), ((), ()))

    def window(ref, rs, c0k):
        parts = []
        for i in range(WIN_H):
            start = rs * GRID_W + (i * GRID_W + c0k)
            if not isinstance(start, int):
                start = pl.multiple_of(start, 16)
            parts.append(ref[0, pl.ds(start, KEY_BLK), :])
        return jnp.concatenate(parts, axis=0)

    tiles = []
    for r_local, row, d in items:
        if isinstance(row, int):
            rs = min(max(row - WIN_H // 2, 0), rows - WIN_H)
        else:
            rs = jnp.clip(row - WIN_H // 2, 0, rows - WIN_H)
        qf = q_ref[0, pl.ds(_aligned(r_local, GRID_W), GRID_W), :].astype(F32)
        row_off = 0
        for c0q, nq, c0k in QUERY_BLOCKS:
            nrow = nq * HEADS_PER_GROUP
            rep = 1 if nq % 16 == 0 else 2
            qb = jnp.concatenate([qf[c0q:c0q + nq]] * rep, axis=0).astype(BF16)
            qs = jnp.tile(qb, (HEADS_PER_GROUP // rep, 1)) * qm_ref[row_off:row_off + nrow, :]
            s = lax.dot_general(qs, window(k_ref, rs, c0k), nt, preferred_element_type=F32)
            tiles.append((s, rs, d, row_off, nq, c0k))
            row_off += nrow

    probs = []
    for s, rs, d, row_off, nq, c0k in tiles:
        nrow = nq * HEADS_PER_GROUP
        bias_off = (WIN_H - d) * KEY_BLK
        s = s + t_ref[row_off:row_off + nrow, bias_off:bias_off + WIN_H * KEY_BLK]
        m = jnp.max(s, axis=-1, keepdims=True)
        p = jnp.exp(s - m)
        l = jnp.sum(p, axis=-1, keepdims=True)
        probs.append((p * (1.0 / l)).astype(BF16))

    outs = []
    for pn, (s, rs, d, row_off, nq, c0k) in zip(probs, tiles):
        o = _dot(pn, window(v_ref, rs, c0k))
        lane_head = lax.broadcasted_iota(jnp.int32, (nq, MXU_N), 1) // NA_HEAD_DIM
        acc = o[(HEADS_PER_GROUP - 1) * nq:]
        for hh in range(HEADS_PER_GROUP - 2, -1, -1):
            acc = jnp.where(lane_head == hh, o[hh * nq:(hh + 1) * nq], acc)
        outs.append(acc)

    nb = len(QUERY_BLOCKS)
    for n, (r_local, row, d) in enumerate(items):
        out = jnp.concatenate(outs[n * nb:(n + 1) * nb], axis=0)
        o_ref[0, pl.ds(_aligned(r_local, GRID_W), GRID_W), :] = out.astype(BF16)


def _attn_kernel(q_ref, k_ref, v_ref, t_ref, qm_ref, o_ref, *, rows, rows_per_step):
    half = WIN_H // 2
    rc = pl.program_id(2)
    n_rc = pl.num_programs(2)
    row0 = rc * rows_per_step
    body = functools.partial(_attn_rows, q_ref=q_ref, k_ref=k_ref, v_ref=v_ref, t_ref=t_ref,
                             qm_ref=qm_ref, o_ref=o_ref, rows=rows)
    edge = -(-half // ROWS_PER_ITER) * ROWS_PER_ITER
    lo = jnp.where(rc == 0, edge, 0)
    hi = jnp.where(rc == n_rc - 1, rows_per_step - edge, rows_per_step)

    def interior(it, carry):
        base = it * ROWS_PER_ITER
        body([(base + u, row0 + base + u, half) for u in range(ROWS_PER_ITER)])
        return carry

    lax.fori_loop(lo // ROWS_PER_ITER, hi // ROWS_PER_ITER, interior, 0)

    def window_offset(r):
        return r - min(max(r - half, 0), rows - WIN_H)

    def static_rows(first, local_shift):
        for r0 in range(first, first + edge, ROWS_PER_ITER):
            body([(r - local_shift, r, window_offset(r)) for r in range(r0, r0 + ROWS_PER_ITER)])

    @pl.when(rc == 0)
    def _():
        static_rows(0, 0)

    @pl.when(rc == n_rc - 1)
    def _():
        static_rows(rows - edge, rows - rows_per_step)


def _score_row_layout():
    h, c, c0 = [], [], []
    for c0q, nq, c0k in QUERY_BLOCKS:
        for hh in range(HEADS_PER_GROUP):
            h += [hh] * nq
            c += list(range(c0q, c0q + nq))
            c0 += [c0k] * nq
    return np.array(h), np.array(c), np.array(c0)


def _bias_table(rpb):
    n_dr = 2 * WIN_H - 1
    n_hg = rpb.shape[0] // HEADS_PER_GROUP
    h, c, c0 = _score_row_layout()
    start = np.clip(c - WIN_W // 2, 0, GRID_W - WIN_W)
    kc = c0[:, None] + np.arange(KEY_BLK)[None, :]
    inside = (kc >= start[:, None]) & (kc < start[:, None] + WIN_W)
    dc = np.clip(kc - c[:, None] + WIN_W - 1, 0, 2 * WIN_W - 2)
    heads = np.arange(n_hg)[:, None] * HEADS_PER_GROUP + h[None, :]
    t = rpb[heads[:, :, None, None], np.arange(n_dr)[None, None, :, None], dc[None, :, None, :]]
    t = jnp.where(inside[None, :, None, :], t, MASK_VALUE)
    t = t.reshape(n_hg, ATTN_ROWS, n_dr * KEY_BLK)
    return jnp.pad(t, ((0, 0), (0, 0), (KEY_BLK, 0)), constant_values=MASK_VALUE)


def _head_mask():
    h, _, _ = _score_row_layout()
    return jnp.asarray(h[:, None] == (np.arange(MXU_N) // NA_HEAD_DIM)[None, :], BF16)


def _attention(qkv, table, batch, seq, d):
    rows = seq // GRID_W
    rps = _pick(rows, (32, 16, 8))
    assert rps % ROWS_PER_ITER == 0 and rows >= 2 * WIN_H and rps >= 2 * ROWS_PER_ITER
    qkv3 = qkv.reshape(batch, seq, 3 * d)
    n_hg = d // MXU_N
    tw = table.shape[-1]
    out = pl.pallas_call(
        functools.partial(_attn_kernel, rows=rows, rows_per_step=rps),
        grid=(batch, n_hg, rows // rps),
        in_specs=[
            pl.BlockSpec((1, rps * GRID_W, MXU_N), lambda b, g, r: (b, r, g)),
            pl.BlockSpec((1, seq, MXU_N), lambda b, g, r: (b, 0, n_hg + g)),
            pl.BlockSpec((1, seq, MXU_N), lambda b, g, r: (b, 0, 2 * n_hg + g)),
            pl.BlockSpec((None, ATTN_ROWS, tw), lambda b, g, r: (g, 0, 0)),
            pl.BlockSpec((ATTN_ROWS, MXU_N), lambda b, g, r: (0, 0)),
        ],
        out_specs=pl.BlockSpec((1, rps * GRID_W, MXU_N), lambda b, g, r: (b, r, g)),
        out_shape=jax.ShapeDtypeStruct((batch, seq, d), BF16),
        compiler_params=_cparams("parallel", "parallel", "arbitrary"),
        name="na_attn",
    )(qkv3, qkv3, qkv3, table, _head_mask())
    return out.reshape(batch * seq, d)


def _proj_kernel(a_ref, w_ref, x_ref, o_ref):
    o_ref[...] = x_ref[...] + _dot(a_ref[...], w_ref[...])


def _proj_residual(a, w, layer, x):
    m, k = a.shape
    n = w.shape[2]
    tm = _pick(m, (1024, 512, 256, 128))
    tn = _pick(n, (1024, 512, 256, 128))
    return pl.pallas_call(
        _proj_kernel,
        grid=(m // tm, n // tn),
        in_specs=[
            pl.BlockSpec((tm, k), lambda i, j: (i, 0)),
            pl.BlockSpec((None, k, tn), lambda i, j: (layer, 0, j)),
            pl.BlockSpec((tm, tn), lambda i, j: (i, j)),
        ],
        out_specs=pl.BlockSpec((tm, tn), lambda i, j: (i, j)),
        out_shape=jax.ShapeDtypeStruct((m, n), F32),
        compiler_params=_cparams("parallel", "arbitrary"),
        name="proj_residual",
    )(a, w, x)


def _na_mixer(x1, hmix, w_qkv, q_gain, k_gain, rpb, w_o, layer, batch, seq):
    d = x1.shape[1]
    n_heads = d // NA_HEAD_DIM
    gains = jnp.concatenate([jnp.tile(q_gain * (NA_HEAD_DIM ** -0.5), n_heads),
                             jnp.tile(k_gain, n_heads),
                             jnp.ones((d,), F32)]).reshape(1, 3 * d)
    qkv = _qkv(hmix, w_qkv, layer, gains)
    att = _attention(qkv, _bias_table(rpb), batch, seq, d)
    return _proj_residual(att, w_o, layer, x1)


def _gelu_exact(x):
    return 0.5 * x * (1.0 + lax.erf(x * (1.0 / math.sqrt(2.0))))


def _sgu_in_kernel(h_ref, w_ref, z_ref, ss_ref, *, n_u_blocks):
    j = pl.program_id(1)
    z = _gelu_exact(_dot(h_ref[...], w_ref[...]))
    z_ref[...] = z.astype(BF16)

    @pl.when(j == 0)
    def _():
        ss_ref[...] = jnp.zeros_like(ss_ref)

    @pl.when(j >= n_u_blocks)
    def _():
        ss_ref[...] += jnp.sum(z * z, axis=-1, keepdims=True)


def _sgu_in(h, w, layer):
    m, d = h.shape
    n = w.shape[2]
    tm = _pick(m, (1024, 512, 256, 128))
    tn = _pick(n // 2, (1024, 512, 256, 128))
    return pl.pallas_call(
        functools.partial(_sgu_in_kernel, n_u_blocks=n // 2 // tn),
        grid=(m // tm, n // tn),
        in_specs=[
            pl.BlockSpec((tm, d), lambda i, j: (i, 0)),
            pl.BlockSpec((None, d, tn), lambda i, j: (layer, 0, j)),
        ],
        out_specs=[
            pl.BlockSpec((tm, tn), lambda i, j: (i, j)),
            pl.BlockSpec((tm, LANES), lambda i, j: (i, 0)),
        ],
        out_shape=[jax.ShapeDtypeStruct((m, n), BF16), jax.ShapeDtypeStruct((m, LANES), F32)],
        compiler_params=_cparams("parallel", "arbitrary"),
        name="sgu_in",
    )(h, w)


def _sgu_out_kernel(u_ref, v_ref, ss_ref, vg_ref, ws_ref, bs_ref, wo_ref, x_ref, o_ref, t_ref,
                    *, d_sgu):
    tm = u_ref.shape[0]
    gd = d_sgu // SGU_GROUPS
    rinv = lax.rsqrt(ss_ref[:, 0:1] * (1.0 / d_sgu) + EPS)
    for g in range(SGU_GROUPS):
        sl = slice(g * gd, (g + 1) * gd)
        vn = (v_ref[:, sl].astype(F32) * rinv * vg_ref[:, sl]).astype(BF16)
        ws = ws_ref[g]
        bs = bs_ref[g]
        for c in range(tm // CHUNK):
            rs = slice(c * CHUNK, (c + 1) * CHUNK)
            mixed = _dot(ws, vn[rs]) + bs
            t_ref[rs, sl] = (u_ref[rs, sl].astype(F32) * mixed).astype(BF16)
    o_ref[...] = x_ref[...] + _dot(t_ref[...], wo_ref[...])


def _sgu_out(z, ss, v_gain, w_s, b_s, w_out, layer, x):
    m, d = x.shape
    d_sgu = z.shape[1] // 2
    tm = _pick(m, (256, 128))
    const2 = lambda i: (0, 0)
    const3 = lambda i: (0, 0, 0)
    return pl.pallas_call(
        functools.partial(_sgu_out_kernel, d_sgu=d_sgu),
        grid=(m // tm,),
        in_specs=[
            pl.BlockSpec((tm, d_sgu), lambda i: (i, 0)),
            pl.BlockSpec((tm, d_sgu), lambda i: (i, 1)),
            pl.BlockSpec((tm, LANES), lambda i: (i, 0)),
            pl.BlockSpec((1, d_sgu), const2),
            pl.BlockSpec((SGU_GROUPS, CHUNK, CHUNK), const3),
            pl.BlockSpec((SGU_GROUPS, CHUNK, 1), const3),
            pl.BlockSpec((None, d_sgu, d), lambda i: (layer, 0, 0), pipeline_mode=pl.Buffered(1)),
            pl.BlockSpec((tm, d), lambda i: (i, 0)),
        ],
        out_specs=pl.BlockSpec((tm, d), lambda i: (i, 0)),
        out_shape=jax.ShapeDtypeStruct((m, d), F32),
        scratch_shapes=[pltpu.VMEM((tm, d_sgu), BF16)],
        compiler_params=_cparams("parallel"),
        name="sgu_out",
    )(z, z, ss, v_gain.reshape(1, d_sgu), w_s, b_s.reshape(SGU_GROUPS, CHUNK, 1), w_out, x)


def _sgu_mixer(x1, hmix, w_in, v_gain, w_s, b_s, w_out, layer):
    z, ss = _sgu_in(hmix, w_in, layer)
    return _sgu_out(z, ss, v_gain, w_s, b_s, w_out, layer, x1)


def _pool_kernel(xp_ref, xc_ref, xn_ref, g_ref, w_ref, sc_ref, o_ref, *, seq):
    tm, d = xc_ref.shape
    gd = d // len(POOL_WINDOWS)
    gain = g_ref[...]
    xc = xc_ref[...]
    hc = _rms(xc, gain)
    hext = jnp.concatenate([_rms(xp_ref[...], gain), hc, _rms(xn_ref[...], gain)], axis=0)
    hext = hext.astype(BF16)
    pos0 = (pl.program_id(0) * tm) % seq
    t = lax.broadcasted_iota(jnp.int32, (tm, tm + 2 * POOL_HALO), 0)
    off = lax.broadcasted_iota(jnp.int32, (tm, tm + 2 * POOL_HALO), 1) - POOL_HALO - t
    key_pos = pos0 + t + off
    in_seq = (key_pos >= 0) & (key_pos < seq)
    tq = pos0 + lax.broadcasted_iota(jnp.int32, (tm, 1), 0)
    for g, w in enumerate(POOL_WINDOWS):
        sl = slice(g * gd, (g + 1) * gd)
        band = ((off >= -(w // 2)) & (off < w // 2) & in_seq).astype(BF16)
        cnt = (jnp.clip(tq + w // 2, 0, seq) - jnp.clip(tq - w // 2, 0, seq)).astype(F32)
        p = _dot(band, hext[:, sl]) / cnt - hc[:, sl]
        y = _dot(p.astype(BF16), w_ref[g])
        o_ref[:, sl] = xc[:, sl] + y * sc_ref[:, sl]


def _pool_mixer(x, gain, w_grp, scale, seq):
    m, d = x.shape
    tm = _pick(seq, (512, 256, 128))
    nb = tm // POOL_HALO
    last = m // POOL_HALO - 1
    ng, gd, _ = w_grp.shape
    return pl.pallas_call(
        functools.partial(_pool_kernel, seq=seq),
        grid=(m // tm,),
        in_specs=[
            pl.BlockSpec((POOL_HALO, d), lambda i: (jnp.maximum(i * nb - 1, 0), 0)),
            pl.BlockSpec((tm, d), lambda i: (i, 0)),
            pl.BlockSpec((POOL_HALO, d), lambda i: (jnp.minimum((i + 1) * nb, last), 0)),
            pl.BlockSpec((1, d), lambda i: (0, 0)),
            pl.BlockSpec((ng, gd, gd), lambda i: (0, 0, 0)),
            pl.BlockSpec((1, d), lambda i: (0, 0)),
        ],
        out_specs=pl.BlockSpec((tm, d), lambda i: (i, 0)),
        out_shape=jax.ShapeDtypeStruct((m, d), F32),
        compiler_params=_cparams("parallel"),
        name="pool_mixer",
    )(x, x, x, gain.reshape(1, d), w_grp, scale.reshape(1, d))


def kernel(x, ffn1_norm, ffn1_w_gate, ffn1_w_up, ffn1_w_down, mix_norm, ffn2_norm, ffn2_w_gate, ffn2_w_up, ffn2_w_down, out_norm, na_w_qkv, na_q_gain, na_k_gain, na_rpb, na_w_o, sgu_w_in, sgu_v_gain, sgu_w_s, sgu_b_s, sgu_w_out, pool_w, pool_scale):
    batch, seq, d = x.shape
    depth = ffn1_norm.shape[0]
    f1g, f1u, f1d = _cast_bf16(ffn1_w_gate), _cast_bf16(ffn1_w_up), _cast_bf16(ffn1_w_down)
    f2g, f2u, f2d = _cast_bf16(ffn2_w_gate), _cast_bf16(ffn2_w_up), _cast_bf16(ffn2_w_down)
    w_qkv, w_o = _cast_bf16(na_w_qkv), _cast_bf16(na_w_o)
    w_in, w_out = _cast_bf16(sgu_w_in), _cast_bf16(sgu_w_out)
    xs = x.reshape(batch * seq, d)
    for i in range(depth):
        kind, j = i % N_MIXERS, i // N_MIXERS
        emit_mix = kind != 2
        res = _ffn(xs, ffn1_norm[i], f1g, f1u, f1d, i, out_norm[i], mix_norm[i],
                   post_norm=False, emit_mix=emit_mix)
        if kind == 0:
            x1, hmix = res
            x2 = _na_mixer(x1, hmix, w_qkv, na_q_gain[j], na_k_gain[j], na_rpb[j], w_o, j, batch, seq)
        elif kind == 1:
            x1, hmix = res
            x2 = _sgu_mixer(x1, hmix, w_in, sgu_v_gain[j], sgu_w_s[j].astype(BF16), sgu_b_s[j],
                            w_out, j)
        else:
            x2 = _pool_mixer(res, mix_norm[i], pool_w[j].astype(BF16), pool_scale[j], seq)
        xs = _ffn(x2, ffn2_norm[i], f2g, f2u, f2d, i, out_norm[i], mix_norm[i],
                  post_norm=True, emit_mix=False)
    return xs.reshape(batch, seq, d)
```

```python
import functools
import math

import jax
import jax.numpy as jnp
import numpy as np
from jax import lax
from jax.experimental import pallas as pl
from jax.experimental.pallas import tpu as pltpu

EPS = 1e-6
N_MIXERS = 3
NA_HEAD_DIM = 32
GRID_W = 64
WIN_H = 8
WIN_W = 16
CHUNK = 128
SGU_GROUPS = 16
POOL_WINDOWS = (2, 4, 8, 16)
POOL_HALO = 64

LANES = 128
MXU_N = 256
HEADS_PER_GROUP = MXU_N // NA_HEAD_DIM
VMEM_LIMIT = 60000 * 1024
CAST_BLOCK_BYTES = 6 * 1024 * 1024
MASK_VALUE = -1e30

BF16 = jnp.bfloat16
F32 = jnp.float32


def _cparams(*sem):
    return pltpu.CompilerParams(dimension_semantics=sem, vmem_limit_bytes=VMEM_LIMIT)


def _rms(x, g):
    return x * lax.rsqrt(jnp.mean(x * x, axis=-1, keepdims=True) + EPS) * g


def _dot(a, b):
    return jnp.dot(a, b, preferred_element_type=F32)


def _aligned(i, m):
    return i * m if isinstance(i, int) else pl.multiple_of(i * m, m)


def _pick(n, prefs):
    for p in prefs:
        if n % p == 0:
            return p
    return n


def _cast_kernel(w_ref, o_ref):
    o_ref[...] = w_ref[...].astype(BF16)


def _cast_bf16(w):
    l, r, c = w.shape
    if l == 0:
        return w.astype(BF16)
    tr = r
    while tr * c * 4 > CAST_BLOCK_BYTES and tr % 2 == 0 and (tr // 2) % 16 == 0:
        tr //= 2
    return pl.pallas_call(
        _cast_kernel,
        grid=(l, r // tr),
        in_specs=[pl.BlockSpec((1, tr, c), lambda a, b: (a, b, 0))],
        out_specs=pl.BlockSpec((1, tr, c), lambda a, b: (a, b, 0)),
        out_shape=jax.ShapeDtypeStruct(w.shape, BF16),
        compiler_params=_cparams("parallel", "parallel"),
        name="cast_bf16",
    )(w)


def _ffn_kernel(x_ref, g_ref, wg_ref, wu_ref, wd_ref, og_ref, mg_ref, y_ref, *rest,
                post_norm, emit_mix):
    if emit_mix:
        (hm_ref,) = rest
        h_ref = hm_ref
    else:
        (h_ref,) = rest
    j = pl.program_id(1)

    def chunk():
        h = h_ref[...]
        gate = _dot(h, wg_ref[...])
        up = _dot(h, wu_ref[...])
        act = (gate * jax.nn.sigmoid(gate) * up * 0.5).astype(BF16)
        return _dot(act, wd_ref[...])

    @pl.when(j == 0)
    def _():
        h_ref[...] = _rms(x_ref[...], g_ref[...]).astype(BF16)
        y_ref[...] = x_ref[...] + chunk()

    @pl.when(j > 0)
    def _():
        y_ref[...] += chunk()

    @pl.when(j == pl.num_programs(1) - 1)
    def _():
        y = y_ref[...]
        if post_norm:
            y = _rms(y, og_ref[...])
            y_ref[...] = y
        if emit_mix:
            hm_ref[...] = _rms(y, mg_ref[...]).astype(BF16)


def _ffn(x, gain, wg, wu, wd, layer, out_gain, mix_gain, *, post_norm, emit_mix):
    m, d = x.shape
    f = wg.shape[2]
    tm = _pick(m, (1024, 512, 256, 128))
    tf = _pick(f, (512, 256, 128))
    row = lambda i, j: (i, 0)
    vec = lambda i, j: (0, 0)
    out_shape = [jax.ShapeDtypeStruct((m, d), F32)]
    out_specs = [pl.BlockSpec((tm, d), row)]
    if emit_mix:
        out_shape.append(jax.ShapeDtypeStruct((m, d), BF16))
        out_specs.append(pl.BlockSpec((tm, d), row))
    res = pl.pallas_call(
        functools.partial(_ffn_kernel, post_norm=post_norm, emit_mix=emit_mix),
        grid=(m // tm, f // tf),
        in_specs=[
            pl.BlockSpec((tm, d), row),
            pl.BlockSpec((1, d), vec),
            pl.BlockSpec((None, d, tf), lambda i, j: (layer, 0, j)),
            pl.BlockSpec((None, d, tf), lambda i, j: (layer, 0, j)),
            pl.BlockSpec((None, tf, d), lambda i, j: (layer, j, 0)),
            pl.BlockSpec((1, d), vec),
            pl.BlockSpec((1, d), vec),
        ],
        out_specs=out_specs,
        out_shape=out_shape,
        scratch_shapes=[] if emit_mix else [pltpu.VMEM((tm, d), BF16)],
        compiler_params=_cparams("parallel", "arbitrary"),
        name="ffn",
    )(x, gain.reshape(1, d), wg, wu, wd, out_gain.reshape(1, d), mix_gain.reshape(1, d))
    return res if emit_mix else res[0]


def _qkv_kernel(h_ref, w_ref, seg_ref, gain_ref, o_ref, *, n_norm_blocks):
    j = pl.program_id(1)

    @pl.when(j < n_norm_blocks)
    def _():
        seg = seg_ref[...]
        h = h_ref[...]
        n_sub = w_ref.shape[1] // MXU_N
        cols = [slice(c * MXU_N, (c + 1) * MXU_N) for c in range(n_sub)]

        def normalise(a, sl):
            sq = a * a
            hi = sq.astype(BF16)
            lo = (sq - hi.astype(F32)).astype(BF16)
            ss = _dot(hi, seg) + _dot(lo, seg)
            y = a * lax.rsqrt(ss * (1.0 / NA_HEAD_DIM) + EPS) * gain_ref[:, sl]
            o_ref[:, sl] = y.astype(BF16)

        a_prev = _dot(h, w_ref[:, cols[0]])
        for c in range(1, n_sub):
            a_next = _dot(h, w_ref[:, cols[c]])
            normalise(a_prev, cols[c - 1])
            a_prev = a_next
        normalise(a_prev, cols[n_sub - 1])

    @pl.when(j >= n_norm_blocks)
    def _():
        o_ref[...] = _dot(h_ref[...], w_ref[...]).astype(BF16)


def _qkv(h, w, layer, gains):
    m, d = h.shape
    n = w.shape[2]
    tm = _pick(m, (1024, 512, 256, 128))
    tn = _pick(d, (1024, 512, 256))
    head = np.arange(MXU_N) // NA_HEAD_DIM
    seg = jnp.asarray(head[:, None] == head[None, :], BF16)
    return pl.pallas_call(
        functools.partial(_qkv_kernel, n_norm_blocks=2 * d // tn),
        grid=(m // tm, n // tn),
        in_specs=[
            pl.BlockSpec((tm, d), lambda i, j: (i, 0)),
            pl.BlockSpec((None, d, tn), lambda i, j: (layer, 0, j)),
            pl.BlockSpec((MXU_N, MXU_N), lambda i, j: (0, 0)),
            pl.BlockSpec((1, tn), lambda i, j: (0, j)),
        ],
        out_specs=pl.BlockSpec((tm, tn), lambda i, j: (i, j)),
        out_shape=jax.ShapeDtypeStruct((m, n), BF16),
        compiler_params=_cparams("parallel", "arbitrary"),
        name="na_qkv",
    )(h, w, seg, gains)


KEY_BLK = 32
QUERY_BLOCKS = ((0, 24, 0), (24, 16, 16), (40, 24, 32))
ATTN_ROWS = HEADS_PER_GROUP * GRID_W
ROWS_PER_ITER = 4


def _check_query_blocks():
    cols = np.arange(GRID_W)
    start = np.clip(cols - WIN_W // 2, 0, GRID_W - WIN_W)
    covered = 0
    for c0q, nq, c0k in QUERY_BLOCKS:
        assert c0q == covered and c0k % 16 == 0 and nq % 8 == 0
        assert (start[c0q:c0q + nq] >= c0k).all() and (start[c0q:c0q + nq] + WIN_W <= c0k + KEY_BLK).all()
        covered += nq
    assert covered == GRID_W


_check_query_blocks()


def _attn_rows(items, q_ref, k_ref, v_ref, t_ref, qm_ref, o_ref, *, rows):
    nt = (((1,), (1,)), ((), ()))

    def window(ref, rs, c0k):
        parts = []
        for i in range(WIN_H):
            start = rs * GRID_W + (i * GRID_W + c0k)
            if not isinstance(start, int):
                start = pl.multiple_of(start, 16)
            parts.append(ref[0, pl.ds(start, KEY_BLK), :])
        return jnp.concatenate(parts, axis=0)

    tiles = []
    for r_local, row, d in items:
        if isinstance(row, int):
            rs = min(max(row - WIN_H // 2, 0), rows - WIN_H)
        else:
            rs = jnp.clip(row - WIN_H // 2, 0, rows - WIN_H)
        qf = q_ref[0, pl.ds(_aligned(r_local, GRID_W), GRID_W), :].astype(F32)
        row_off = 0
        for c0q, nq, c0k in QUERY_BLOCKS:
            nrow = nq * HEADS_PER_GROUP
            rep = 1 if nq % 16 == 0 else 2
            qb = jnp.concatenate([qf[c0q:c0q + nq]] * rep, axis=0).astype(BF16)
            qs = jnp.tile(qb, (HEADS_PER_GROUP // rep, 1)) * qm_ref[row_off:row_off + nrow, :]
            s = lax.dot_general(qs, window(k_ref, rs, c0k), nt, preferred_element_type=F32)
            tiles.append((s, rs, d, row_off, nq, c0k))
            row_off += nrow

    probs = []
    for s, rs, d, row_off, nq, c0k in tiles:
        nrow = nq * HEADS_PER_GROUP
        bias_off = (WIN_H - d) * KEY_BLK
        s = s + t_ref[row_off:row_off + nrow, bias_off:bias_off + WIN_H * KEY_BLK]
        m = jnp.max(s, axis=-1, keepdims=True)
        p = jnp.exp(s - m)
        l = jnp.sum(p, axis=-1, keepdims=True)
        probs.append((p * (1.0 / l)).astype(BF16))

    outs = []
    for pn, (s, rs, d, row_off, nq, c0k) in zip(probs, tiles):
        o = _dot(pn, window(v_ref, rs, c0k))
        lane_head = lax.broadcasted_iota(jnp.int32, (nq, MXU_N), 1) // NA_HEAD_DIM
        acc = o[(HEADS_PER_GROUP - 1) * nq:]
        for hh in range(HEADS_PER_GROUP - 2, -1, -1):
            acc = jnp.where(lane_head == hh, o[hh * nq:(hh + 1) * nq], acc)
        outs.append(acc)

    nb = len(QUERY_BLOCKS)
    for n, (r_local, row, d) in enumerate(items):
        out = jnp.concatenate(outs[n * nb:(n + 1) * nb], axis=0)
        o_ref[0, pl.ds(_aligned(r_local, GRID_W), GRID_W), :] = out.astype(BF16)


def _attn_kernel(q_ref, k_ref, v_ref, t_ref, qm_ref, o_ref, *, rows, rows_per_step):
    half = WIN_H // 2
    rc = pl.program_id(2)
    n_rc = pl.num_programs(2)
    row0 = rc * rows_per_step
    body = functools.partial(_attn_rows, q_ref=q_ref, k_ref=k_ref, v_ref=v_ref, t_ref=t_ref,
                             qm_ref=qm_ref, o_ref=o_ref, rows=rows)
    edge = -(-half // ROWS_PER_ITER) * ROWS_PER_ITER
    lo = jnp.where(rc == 0, edge, 0)
    hi = jnp.where(rc == n_rc - 1, rows_per_step - edge, rows_per_step)

    def interior(it, carry):
        base = it * ROWS_PER_ITER
        body([(base + u, row0 + base + u, half) for u in range(ROWS_PER_ITER)])
        return carry

    lax.fori_loop(lo // ROWS_PER_ITER, hi // ROWS_PER_ITER, interior, 0)

    def window_offset(r):
        return r - min(max(r - half, 0), rows - WIN_H)

    def static_rows(first, local_shift):
        for r0 in range(first, first + edge, ROWS_PER_ITER):
            body([(r - local_shift, r, window_offset(r)) for r in range(r0, r0 + ROWS_PER_ITER)])

    @pl.when(rc == 0)
    def _():
        static_rows(0, 0)

    @pl.when(rc == n_rc - 1)
    def _():
        static_rows(rows - edge, rows - rows_per_step)


def _score_row_layout():
    h, c, c0 = [], [], []
    for c0q, nq, c0k in QUERY_BLOCKS:
        for hh in range(HEADS_PER_GROUP):
            h += [hh] * nq
            c += list(range(c0q, c0q + nq))
            c0 += [c0k] * nq
    return np.array(h), np.array(c), np.array(c0)


def _bias_table(rpb):
    n_dr = 2 * WIN_H - 1
    n_hg = rpb.shape[0] // HEADS_PER_GROUP
    parts = []
    for c0q, nq, c0k in QUERY_BLOCKS:
        c = np.arange(c0q, c0q + nq)
        start = np.clip(c - WIN_W // 2, 0, GRID_W - WIN_W)
        kc = c0k + np.arange(KEY_BLK)[None, :] + 0 * c[:, None]
        inside = (kc >= start[:, None]) & (kc < start[:, None] + WIN_W)
        dc = np.clip(kc - c[:, None] + WIN_W - 1, 0, 2 * WIN_W - 2)
        t = jnp.where(inside[None, None], rpb[:, :, dc], MASK_VALUE)
        t = jnp.transpose(t, (0, 2, 1, 3))
        parts.append(t.reshape(n_hg, HEADS_PER_GROUP * nq, n_dr * KEY_BLK))
    t = jnp.concatenate(parts, axis=1)
    return jnp.pad(t, ((0, 0), (0, 0), (KEY_BLK, 0)), constant_values=MASK_VALUE)


def _head_mask():
    h, _, _ = _score_row_layout()
    return jnp.asarray(h[:, None] == (np.arange(MXU_N) // NA_HEAD_DIM)[None, :], BF16)


def _attention(qkv, table, batch, seq, d):
    rows = seq // GRID_W
    rps = _pick(rows, (32, 16, 8))
    assert rps % ROWS_PER_ITER == 0 and rows >= 2 * WIN_H and rps >= 2 * ROWS_PER_ITER
    qkv3 = qkv.reshape(batch, seq, 3 * d)
    n_hg = d // MXU_N
    tw = table.shape[-1]
    out = pl.pallas_call(
        functools.partial(_attn_kernel, rows=rows, rows_per_step=rps),
        grid=(batch, n_hg, rows // rps),
        in_specs=[
            pl.BlockSpec((1, rps * GRID_W, MXU_N), lambda b, g, r: (b, r, g)),
            pl.BlockSpec((1, seq, MXU_N), lambda b, g, r: (b, 0, n_hg + g)),
            pl.BlockSpec((1, seq, MXU_N), lambda b, g, r: (b, 0, 2 * n_hg + g)),
            pl.BlockSpec((None, ATTN_ROWS, tw), lambda b, g, r: (g, 0, 0)),
            pl.BlockSpec((ATTN_ROWS, MXU_N), lambda b, g, r: (0, 0)),
        ],
        out_specs=pl.BlockSpec((1, rps * GRID_W, MXU_N), lambda b, g, r: (b, r, g)),
        out_shape=jax.ShapeDtypeStruct((batch, seq, d), BF16),
        compiler_params=_cparams("parallel", "parallel", "arbitrary"),
        name="na_attn",
    )(qkv3, qkv3, qkv3, table, _head_mask())
    return out.reshape(batch * seq, d)


def _proj_kernel(a_ref, w_ref, x_ref, o_ref):
    o_ref[...] = x_ref[...] + _dot(a_ref[...], w_ref[...])


def _proj_residual(a, w, layer, x):
    m, k = a.shape
    n = w.shape[2]
    tm = _pick(m, (512, 256, 128))
    tn = n
    return pl.pallas_call(
        _proj_kernel,
        grid=(m // tm, n // tn),
        in_specs=[
            pl.BlockSpec((tm, k), lambda i, j: (i, 0)),
            pl.BlockSpec((None, k, tn), lambda i, j: (layer, 0, j)),
            pl.BlockSpec((tm, tn), lambda i, j: (i, j)),
        ],
        out_specs=pl.BlockSpec((tm, tn), lambda i, j: (i, j)),
        out_shape=jax.ShapeDtypeStruct((m, n), F32),
        compiler_params=_cparams("parallel", "arbitrary"),
        name="proj_residual",
    )(a, w, x)


def _na_mixer(x1, hmix, w_qkv, q_gain, k_gain, rpb, w_o, layer, batch, seq):
    d = x1.shape[1]
    n_heads = d // NA_HEAD_DIM
    gains = jnp.concatenate([jnp.tile(q_gain * (NA_HEAD_DIM ** -0.5), n_heads),
                             jnp.tile(k_gain, n_heads),
                             jnp.ones((d,), F32)]).reshape(1, 3 * d)
    qkv = _qkv(hmix, w_qkv, layer, gains)
    att = _attention(qkv, _bias_table(rpb), batch, seq, d)
    return _proj_residual(att, w_o, layer, x1)


def _gelu_exact(x):
    return 0.5 * x * (1.0 + lax.erf(x * (1.0 / math.sqrt(2.0))))


def _sgu_in_kernel(h_ref, w_ref, z_ref, ss_ref, *, n_u_blocks):
    j = pl.program_id(1)
    z = _gelu_exact(_dot(h_ref[...], w_ref[...]))
    z_ref[...] = z.astype(BF16)

    @pl.when(j == 0)
    def _():
        ss_ref[...] = jnp.zeros_like(ss_ref)

    @pl.when(j >= n_u_blocks)
    def _():
        ss_ref[...] += jnp.sum(z * z, axis=-1, keepdims=True)


def _sgu_in(h, w, layer):
    m, d = h.shape
    n = w.shape[2]
    tm = _pick(m, (1024, 512, 256, 128))
    tn = _pick(n // 2, (1024, 512, 256, 128))
    return pl.pallas_call(
        functools.partial(_sgu_in_kernel, n_u_blocks=n // 2 // tn),
        grid=(m // tm, n // tn),
        in_specs=[
            pl.BlockSpec((tm, d), lambda i, j: (i, 0)),
            pl.BlockSpec((None, d, tn), lambda i, j: (layer, 0, j)),
        ],
        out_specs=[
            pl.BlockSpec((tm, tn), lambda i, j: (i, j)),
            pl.BlockSpec((tm, LANES), lambda i, j: (i, 0)),
        ],
        out_shape=[jax.ShapeDtypeStruct((m, n), BF16), jax.ShapeDtypeStruct((m, LANES), F32)],
        compiler_params=_cparams("parallel", "arbitrary"),
        name="sgu_in",
    )(h, w)


def _sgu_out_kernel(u_ref, v_ref, ss_ref, vg_ref, ws_ref, bs_ref, wo_ref, x_ref, o_ref, t_ref,
                    *, d_sgu):
    tm = u_ref.shape[0]
    gd = d_sgu // SGU_GROUPS
    rinv = lax.rsqrt(ss_ref[:, 0:1] * (1.0 / d_sgu) + EPS)
    for g in range(SGU_GROUPS):
        sl = slice(g * gd, (g + 1) * gd)
        vn = (v_ref[:, sl].astype(F32) * rinv * vg_ref[:, sl]).astype(BF16)
        ws = ws_ref[g]
        bs = bs_ref[g]
        for c in range(tm // CHUNK):
            rs = slice(c * CHUNK, (c + 1) * CHUNK)
            mixed = _dot(ws, vn[rs]) + bs
            t_ref[rs, sl] = (u_ref[rs, sl].astype(F32) * mixed).astype(BF16)
    o_ref[...] = x_ref[...] + _dot(t_ref[...], wo_ref[...])


def _sgu_out(z, ss, v_gain, w_s, b_s, w_out, layer, x):
    m, d = x.shape
    d_sgu = z.shape[1] // 2
    tm = _pick(m, (256, 128))
    const2 = lambda i: (0, 0)
    const3 = lambda i: (0, 0, 0)
    return pl.pallas_call(
        functools.partial(_sgu_out_kernel, d_sgu=d_sgu),
        grid=(m // tm,),
        in_specs=[
            pl.BlockSpec((tm, d_sgu), lambda i: (i, 0)),
            pl.BlockSpec((tm, d_sgu), lambda i: (i, 1)),
            pl.BlockSpec((tm, LANES), lambda i: (i, 0)),
            pl.BlockSpec((1, d_sgu), const2),
            pl.BlockSpec((SGU_GROUPS, CHUNK, CHUNK), const3),
            pl.BlockSpec((SGU_GROUPS, CHUNK, 1), const3),
            pl.BlockSpec((None, d_sgu, d), lambda i: (layer, 0, 0), pipeline_mode=pl.Buffered(1)),
            pl.BlockSpec((tm, d), lambda i: (i, 0)),
        ],
        out_specs=pl.BlockSpec((tm, d), lambda i: (i, 0)),
        out_shape=jax.ShapeDtypeStruct((m, d), F32),
        scratch_shapes=[pltpu.VMEM((tm, d_sgu), BF16)],
        compiler_params=_cparams("parallel"),
        name="sgu_out",
    )(z, z, ss, v_gain.reshape(1, d_sgu), w_s, b_s.reshape(SGU_GROUPS, CHUNK, 1), w_out, x)


def _sgu_mixer(x1, hmix, w_in, v_gain, w_s, b_s, w_out, layer):
    z, ss = _sgu_in(hmix, w_in, layer)
    return _sgu_out(z, ss, v_gain, w_s, b_s, w_out, layer, x1)


def _pool_kernel(xp_ref, xc_ref, xn_ref, g_ref, w_ref, sc_ref, o_ref, *, seq):
    tm, d = xc_ref.shape
    gd = d // len(POOL_WINDOWS)
    gain = g_ref[...]
    xc = xc_ref[...]
    hc = _rms(xc, gain)
    hext = jnp.concatenate([_rms(xp_ref[...], gain), hc, _rms(xn_ref[...], gain)], axis=0)
    hext = hext.astype(BF16)
    pos0 = (pl.program_id(0) * tm) % seq
    t = lax.broadcasted_iota(jnp.int32, (tm, tm + 2 * POOL_HALO), 0)
    off = lax.broadcasted_iota(jnp.int32, (tm, tm + 2 * POOL_HALO), 1) - POOL_HALO - t
    key_pos = pos0 + t + off
    in_seq = (key_pos >= 0) & (key_pos < seq)
    tq = pos0 + lax.broadcasted_iota(jnp.int32, (tm, 1), 0)
    for g, w in enumerate(POOL_WINDOWS):
        sl = slice(g * gd, (g + 1) * gd)
        band = ((off >= -(w // 2)) & (off < w // 2) & in_seq).astype(BF16)
        cnt = (jnp.clip(tq + w // 2, 0, seq) - jnp.clip(tq - w // 2, 0, seq)).astype(F32)
        p = _dot(band, hext[:, sl]) / cnt - hc[:, sl]
        y = _dot(p.astype(BF16), w_ref[g])
        o_ref[:, sl] = xc[:, sl] + y * sc_ref[:, sl]


def _pool_mixer(x, gain, w_grp, scale, seq):
    m, d = x.shape
    tm = _pick(seq, (512, 256, 128))
    nb = tm // POOL_HALO
    last = m // POOL_HALO - 1
    ng, gd, _ = w_grp.shape
    return pl.pallas_call(
        functools.partial(_pool_kernel, seq=seq),
        grid=(m // tm,),
        in_specs=[
            pl.BlockSpec((POOL_HALO, d), lambda i: (jnp.maximum(i * nb - 1, 0), 0)),
            pl.BlockSpec((tm, d), lambda i: (i, 0)),
            pl.BlockSpec((POOL_HALO, d), lambda i: (jnp.minimum((i + 1) * nb, last), 0)),
            pl.BlockSpec((1, d), lambda i: (0, 0)),
            pl.BlockSpec((ng, gd, gd), lambda i: (0, 0, 0)),
            pl.BlockSpec((1, d), lambda i: (0, 0)),
        ],
        out_specs=pl.BlockSpec((tm, d), lambda i: (i, 0)),
        out_shape=jax.ShapeDtypeStruct((m, d), F32),
        compiler_params=_cparams("parallel"),
        name="pool_mixer",
    )(x, x, x, gain.reshape(1, d), w_grp, scale.reshape(1, d))


def kernel(x, ffn1_norm, ffn1_w_gate, ffn1_w_up, ffn1_w_down, mix_norm, ffn2_norm, ffn2_w_gate, ffn2_w_up, ffn2_w_down, out_norm, na_w_qkv, na_q_gain, na_k_gain, na_rpb, na_w_o, sgu_w_in, sgu_v_gain, sgu_w_s, sgu_b_s, sgu_w_out, pool_w, pool_scale):
    batch, seq, d = x.shape
    depth = ffn1_norm.shape[0]
    f1g, f1u, f1d = _cast_bf16(ffn1_w_gate), _cast_bf16(ffn1_w_up), _cast_bf16(ffn1_w_down)
    f2g, f2u, f2d = _cast_bf16(ffn2_w_gate), _cast_bf16(ffn2_w_up), _cast_bf16(ffn2_w_down)
    w_qkv, w_o = _cast_bf16(na_w_qkv), _cast_bf16(na_w_o)
    w_in, w_out = _cast_bf16(sgu_w_in), _cast_bf16(sgu_w_out)
    xs = x.reshape(batch * seq, d)
    for i in range(depth):
        kind, j = i % N_MIXERS, i // N_MIXERS
        emit_mix = kind != 2
        res = _ffn(xs, ffn1_norm[i], f1g, f1u, f1d, i, out_norm[i], mix_norm[i],
                   post_norm=False, emit_mix=emit_mix)
        if kind == 0:
            x1, hmix = res
            x2 = _na_mixer(x1, hmix, w_qkv, na_q_gain[j], na_k_gain[j], na_rpb[j], w_o, j, batch, seq)
        elif kind == 1:
            x1, hmix = res
            x2 = _sgu_mixer(x1, hmix, w_in, sgu_v_gain[j], sgu_w_s[j].astype(BF16), sgu_b_s[j],
                            w_out, j)
        else:
            x2 = _pool_mixer(res, mix_norm[i], pool_w[j].astype(BF16), pool_scale[j], seq)
        xs = _ffn(x2, ffn2_norm[i], f2g, f2u, f2d, i, out_norm[i], mix_norm[i],
                  post_norm=True, emit_mix=False)
    return xs.reshape(batch, seq, d)
```

```python
import functools
import math

import jax
import jax.numpy as jnp
import numpy as np
from jax import lax
from jax.experimental import pallas as pl
from jax.experimental.pallas import tpu as pltpu

EPS = 1e-6
N_MIXERS = 3
NA_HEAD_DIM = 32
GRID_W = 64
WIN_H = 8
WIN_W = 16
CHUNK = 128
SGU_GROUPS = 16
POOL_WINDOWS = (2, 4, 8, 16)
POOL_HALO = 64

LANES = 128
MXU_N = 256
HEADS_PER_GROUP = MXU_N // NA_HEAD_DIM
VMEM_LIMIT = 60000 * 1024
CAST_BLOCK_BYTES = 6 * 1024 * 1024
MASK_VALUE = -1e30

BF16 = jnp.bfloat16
F32 = jnp.float32


def _cparams(*sem):
    return pltpu.CompilerParams(dimension_semantics=sem, vmem_limit_bytes=VMEM_LIMIT)


def _rms(x, g):
    return x * lax.rsqrt(jnp.mean(x * x, axis=-1, keepdims=True) + EPS) * g


def _dot(a, b):
    return jnp.dot(a, b, preferred_element_type=F32)


def _aligned(i, m):
    return i * m if isinstance(i, int) else pl.multiple_of(i * m, m)


def _pick(n, prefs):
    for p in prefs:
        if n % p == 0:
            return p
    return n


def _cast_kernel(w_ref, o_ref):
    o_ref[...] = w_ref[...].astype(BF16)


def _cast_bf16(w):
    l, r, c = w.shape
    if l == 0:
        return w.astype(BF16)
    tr = r
    while tr * c * 4 > CAST_BLOCK_BYTES and tr % 2 == 0 and (tr // 2) % 16 == 0:
        tr //= 2
    return pl.pallas_call(
        _cast_kernel,
        grid=(l, r // tr),
        in_specs=[pl.BlockSpec((1, tr, c), lambda a, b: (a, b, 0))],
        out_specs=pl.BlockSpec((1, tr, c), lambda a, b: (a, b, 0)),
        out_shape=jax.ShapeDtypeStruct(w.shape, BF16),
        compiler_params=_cparams("parallel", "parallel"),
        name="cast_bf16",
    )(w)


def _cast_pair_kernel(a_ref, b_ref, o_ref):
    n = a_ref.shape[-1]
    o_ref[0, 0, :, :n] = a_ref[0].astype(BF16)
    o_ref[0, 0, :, n:] = b_ref[0].astype(BF16)


def _cast_gate_up(wg, wu, tf):
    l, d, f = wg.shape
    tr = _pick(d, (1024, 512, 256, 128))
    return pl.pallas_call(
        _cast_pair_kernel,
        grid=(l, f // tf, d // tr),
        in_specs=[pl.BlockSpec((1, tr, tf), lambda a, j, r: (a, r, j)),
                  pl.BlockSpec((1, tr, tf), lambda a, j, r: (a, r, j))],
        out_specs=pl.BlockSpec((1, 1, tr, 2 * tf), lambda a, j, r: (a, j, r, 0)),
        out_shape=jax.ShapeDtypeStruct((l, f // tf, d, 2 * tf), BF16),
        compiler_params=_cparams("parallel", "parallel", "parallel"),
        name="cast_gate_up",
    )(wg, wu)


def _ffn_kernel(x_ref, g_ref, wgu_ref, wd_ref, og_ref, mg_ref, y_ref, *rest,
                post_norm, emit_mix, n_chunks):
    if emit_mix:
        (hm_ref,) = rest
        h_ref = hm_ref
    else:
        (h_ref,) = rest
    j = pl.program_id(1)

    def chunk():
        tf = wd_ref.shape[0]
        gu = _dot(h_ref[...], wgu_ref[...])
        gate, up = gu[:, :tf], gu[:, tf:]
        act = (gate * jax.nn.sigmoid(gate) * up * 0.5).astype(BF16)
        return _dot(act, wd_ref[...])

    def step(first, last):
        if first:
            h_ref[...] = _rms(x_ref[...], g_ref[...]).astype(BF16)
        y = (x_ref[...] if first else y_ref[...]) + chunk()
        if last and post_norm:
            y = _rms(y, og_ref[...])
        y_ref[...] = y
        if last and emit_mix:
            hm_ref[...] = _rms(y, mg_ref[...]).astype(BF16)

    if n_chunks == 1:
        step(True, True)
    else:
        pl.when(j == 0)(lambda: step(True, False))
        pl.when(j == n_chunks - 1)(lambda: step(False, True))
        if n_chunks > 2:
            pl.when((j > 0) & (j < n_chunks - 1))(lambda: step(False, False))


def _ffn_chunk(f):
    return _pick(f, (512, 256, 128))


def _ffn(x, gain, wgu, wd, layer, out_gain, mix_gain, *, post_norm, emit_mix):
    m, d = x.shape
    f = wd.shape[1]
    tm = _pick(m, (1024, 512, 256, 128))
    tf = _ffn_chunk(f)
    row = lambda i, j: (i, 0)
    vec = lambda i, j: (0, 0)
    out_shape = [jax.ShapeDtypeStruct((m, d), F32)]
    out_specs = [pl.BlockSpec((tm, d), row)]
    if emit_mix:
        out_shape.append(jax.ShapeDtypeStruct((m, d), BF16))
        out_specs.append(pl.BlockSpec((tm, d), row))
    res = pl.pallas_call(
        functools.partial(_ffn_kernel, post_norm=post_norm, emit_mix=emit_mix, n_chunks=f // tf),
        grid=(m // tm, f // tf),
        in_specs=[
            pl.BlockSpec((tm, d), row),
            pl.BlockSpec((1, d), vec),
            pl.BlockSpec((None, None, d, 2 * tf), lambda i, j: (layer, j, 0, 0)),
            pl.BlockSpec((None, tf, d), lambda i, j: (layer, j, 0)),
            pl.BlockSpec((1, d), vec),
            pl.BlockSpec((1, d), vec),
        ],
        out_specs=out_specs,
        out_shape=out_shape,
        scratch_shapes=[] if emit_mix else [pltpu.VMEM((tm, d), BF16)],
        compiler_params=_cparams("parallel", "arbitrary"),
        name="ffn",
    )(x, gain.reshape(1, d), wgu, wd, out_gain.reshape(1, d), mix_gain.reshape(1, d))
    return res if emit_mix else res[0]


def _qkv_kernel(h_ref, w_ref, seg_ref, gain_ref, o_ref, *, n_norm_blocks):
    j = pl.program_id(1)

    @pl.when(j < n_norm_blocks)
    def _():
        seg = seg_ref[...]
        h = h_ref[...]
        n_sub = w_ref.shape[1] // MXU_N
        cols = [slice(c * MXU_N, (c + 1) * MXU_N) for c in range(n_sub)]

        def normalise(a, sl):
            sq = a * a
            hi = sq.astype(BF16)
            lo = (sq - hi.astype(F32)).astype(BF16)
            ss = _dot(hi, seg) + _dot(lo, seg)
            y = a * lax.rsqrt(ss * (1.0 / NA_HEAD_DIM) + EPS) * gain_ref[:, sl]
            o_ref[:, sl] = y.astype(BF16)

        a_prev = _dot(h, w_ref[:, cols[0]])
        for c in range(1, n_sub):
            a_next = _dot(h, w_ref[:, cols[c]])
            normalise(a_prev, cols[c - 1])
            a_prev = a_next
        normalise(a_prev, cols[n_sub - 1])

    @pl.when(j >= n_norm_blocks)
    def _():
        o_ref[...] = _dot(h_ref[...], w_ref[...]).astype(BF16)


def _qkv(h, w, layer, gains):
    m, d = h.shape
    n = w.shape[2]
    tm = _pick(m, (1024, 512, 256, 128))
    tn = _pick(d, (1024, 512, 256))
    head = np.arange(MXU_N) // NA_HEAD_DIM
    seg = jnp.asarray(head[:, None] == head[None, :], BF16)
    return pl.pallas_call(
        functools.partial(_qkv_kernel, n_norm_blocks=2 * d // tn),
        grid=(m // tm, n // tn),
        in_specs=[
            pl.BlockSpec((tm, d), lambda i, j: (i, 0)),
            pl.BlockSpec((None, d, tn), lambda i, j: (layer, 0, j)),
            pl.BlockSpec((MXU_N, MXU_N), lambda i, j: (0, 0)),
            pl.BlockSpec((1, tn), lambda i, j: (0, j)),
        ],
        out_specs=pl.BlockSpec((tm, tn), lambda i, j: (i, j)),
        out_shape=jax.ShapeDtypeStruct((m, n), BF16),
        compiler_params=_cparams("parallel", "arbitrary"),
        name="na_qkv",
    )(h, w, seg, gains)


KEY_BLK = 32
QUERY_BLOCKS = ((0, 24, 0), (24, 16, 16), (40, 24, 32))
ATTN_ROWS = HEADS_PER_GROUP * GRID_W
ROWS_PER_ITER = 8


def _check_query_blocks():
    cols = np.arange(GRID_W)
    start = np.clip(cols - WIN_W // 2, 0, GRID_W - WIN_W)
    covered = 0
    for c0q, nq, c0k in QUERY_BLOCKS:
        assert c0q == covered and c0k % 16 == 0 and nq % 8 == 0
        assert (start[c0q:c0q + nq] >= c0k).all() and (start[c0q:c0q + nq] + WIN_W <= c0k + KEY_BLK).all()
        covered += nq
    assert covered == GRID_W


_check_query_blocks()


def _attn_rows(items, q_ref, k_ref, v_ref, t_ref, qm_ref, o_ref, *, rows):
    nt = (((1,), (1,)), ((), ()))

    def window(ref, rs, c0k):
        parts = []
        for i in range(WIN_H):
            start = rs * GRID_W + (i * GRID_W + c0k)
            if not isinstance(start, int):
                start = pl.multiple_of(start, 16)
            parts.append(ref[0, pl.ds(start, KEY_BLK), :])
        return jnp.concatenate(parts, axis=0)

    tiles = []
    for r_local, row, d in items:
        if isinstance(row, int):
            rs = min(max(row - WIN_H // 2, 0), rows - WIN_H)
        else:
            rs = jnp.clip(row - WIN_H // 2, 0, rows - WIN_H)
        qf = q_ref[0, pl.ds(_aligned(r_local, GRID_W), GRID_W), :].astype(F32)
        row_off = 0
        for c0q, nq, c0k in QUERY_BLOCKS:
            nrow = nq * HEADS_PER_GROUP
            rep = 1 if nq % 16 == 0 else 2
            qb = jnp.concatenate([qf[c0q:c0q + nq]] * rep, axis=0).astype(BF16)
            qs = jnp.tile(qb, (HEADS_PER_GROUP // rep, 1)) * qm_ref[row_off:row_off + nrow, :]
            s = lax.dot_general(qs, window(k_ref, rs, c0k), nt, preferred_element_type=F32)
            tiles.append((s, rs, d, row_off, nq, c0k))
            row_off += nrow

    probs = []
    for s, rs, d, row_off, nq, c0k in tiles:
        nrow = nq * HEADS_PER_GROUP
        bias_off = (WIN_H - d) * KEY_BLK
        s = s + t_ref[row_off:row_off + nrow, bias_off:bias_off + WIN_H * KEY_BLK]
        m = jnp.max(s, axis=-1, keepdims=True)
        p = jnp.exp(s - m)
        l = jnp.sum(p, axis=-1, keepdims=True)
        probs.append((p * (1.0 / l)).astype(BF16))

    outs = []
    for pn, (s, rs, d, row_off, nq, c0k) in zip(probs, tiles):
        o = _dot(pn, window(v_ref, rs, c0k))
        lane_head = lax.broadcasted_iota(jnp.int32, (nq, MXU_N), 1) // NA_HEAD_DIM
        acc = o[(HEADS_PER_GROUP - 1) * nq:]
        for hh in range(HEADS_PER_GROUP - 2, -1, -1):
            acc = jnp.where(lane_head == hh, o[hh * nq:(hh + 1) * nq], acc)
        outs.append(acc)

    nb = len(QUERY_BLOCKS)
    for n, (r_local, row, d) in enumerate(items):
        out = jnp.concatenate(outs[n * nb:(n + 1) * nb], axis=0)
        o_ref[0, pl.ds(_aligned(r_local, GRID_W), GRID_W), :] = out.astype(BF16)


def _attn_kernel(q_ref, k_ref, v_ref, t_ref, qm_ref, o_ref, *, rows, rows_per_step):
    half = WIN_H // 2
    rc = pl.program_id(2)
    n_rc = pl.num_programs(2)
    row0 = rc * rows_per_step
    body = functools.partial(_attn_rows, q_ref=q_ref, k_ref=k_ref, v_ref=v_ref, t_ref=t_ref,
                             qm_ref=qm_ref, o_ref=o_ref, rows=rows)
    edge = -(-half // ROWS_PER_ITER) * ROWS_PER_ITER
    lo = jnp.where(rc == 0, edge, 0)
    hi = jnp.where(rc == n_rc - 1, rows_per_step - edge, rows_per_step)

    def interior(it, carry):
        base = it * ROWS_PER_ITER
        body([(base + u, row0 + base + u, half) for u in range(ROWS_PER_ITER)])
        return carry

    lax.fori_loop(lo // ROWS_PER_ITER, hi // ROWS_PER_ITER, interior, 0)

    def window_offset(r):
        return r - min(max(r - half, 0), rows - WIN_H)

    def static_rows(first, local_shift):
        for r0 in range(first, first + edge, ROWS_PER_ITER):
            body([(r - local_shift, r, window_offset(r)) for r in range(r0, r0 + ROWS_PER_ITER)])

    @pl.when(rc == 0)
    def _():
        static_rows(0, 0)

    @pl.when(rc == n_rc - 1)
    def _():
        static_rows(rows - edge, rows - rows_per_step)


def _score_row_layout():
    h, c, c0 = [], [], []
    for c0q, nq, c0k in QUERY_BLOCKS:
        for hh in range(HEADS_PER_GROUP):
            h += [hh] * nq
            c += list(range(c0q, c0q + nq))
            c0 += [c0k] * nq
    return np.array(h), np.array(c), np.array(c0)


def _bias_table(rpb):
    n_dr = 2 * WIN_H - 1
    n_hg = rpb.shape[0] // HEADS_PER_GROUP
    parts = []
    for c0q, nq, c0k in QUERY_BLOCKS:
        c = np.arange(c0q, c0q + nq)
        start = np.clip(c - WIN_W // 2, 0, GRID_W - WIN_W)
        kc = c0k + np.arange(KEY_BLK)[None, :] + 0 * c[:, None]
        inside = (kc >= start[:, None]) & (kc < start[:, None] + WIN_W)
        dc = np.clip(kc - c[:, None] + WIN_W - 1, 0, 2 * WIN_W - 2)
        t = jnp.where(inside[None, None], rpb[:, :, dc], MASK_VALUE)
        t = jnp.transpose(t, (0, 2, 1, 3))
        parts.append(t.reshape(n_hg, HEADS_PER_GROUP * nq, n_dr * KEY_BLK))
    t = jnp.concatenate(parts, axis=1)
    return jnp.pad(t, ((0, 0), (0, 0), (KEY_BLK, 0)), constant_values=MASK_VALUE)


def _head_mask():
    h, _, _ = _score_row_layout()
    return jnp.asarray(h[:, None] == (np.arange(MXU_N) // NA_HEAD_DIM)[None, :], BF16)


def _attention(qkv, table, batch, seq, d):
    rows = seq // GRID_W
    rps = _pick(rows, (32, 16, 8))
    assert rps % ROWS_PER_ITER == 0 and rows >= 2 * WIN_H and rps >= 2 * ROWS_PER_ITER
    qkv3 = qkv.reshape(batch, seq, 3 * d)
    n_hg = d // MXU_N
    tw = table.shape[-1]
    out = pl.pallas_call(
        functools.partial(_attn_kernel, rows=rows, rows_per_step=rps),
        grid=(batch, n_hg, rows // rps),
        in_specs=[
            pl.BlockSpec((1, rps * GRID_W, MXU_N), lambda b, g, r: (b, r, g)),
            pl.BlockSpec((1, seq, MXU_N), lambda b, g, r: (b, 0, n_hg + g)),
            pl.BlockSpec((1, seq, MXU_N), lambda b, g, r: (b, 0, 2 * n_hg + g)),
            pl.BlockSpec((None, ATTN_ROWS, tw), lambda b, g, r: (g, 0, 0)),
            pl.BlockSpec((ATTN_ROWS, MXU_N), lambda b, g, r: (0, 0)),
        ],
        out_specs=pl.BlockSpec((1, rps * GRID_W, MXU_N), lambda b, g, r: (b, r, g)),
        out_shape=jax.ShapeDtypeStruct((batch, seq, d), BF16),
        compiler_params=_cparams("parallel", "parallel", "arbitrary"),
        name="na_attn",
    )(qkv3, qkv3, qkv3, table, _head_mask())
    return out.reshape(batch * seq, d)


def _proj_kernel(a_ref, w_ref, x_ref, o_ref):
    o_ref[...] = x_ref[...] + _dot(a_ref[...], w_ref[...])


def _proj_residual(a, w, layer, x):
    m, k = a.shape
    n = w.shape[2]
    tm = _pick(m, (512, 256, 128))
    tn = n
    return pl.pallas_call(
        _proj_kernel,
        grid=(m // tm, n // tn),
        in_specs=[
            pl.BlockSpec((tm, k), lambda i, j: (i, 0)),
            pl.BlockSpec((None, k, tn), lambda i, j: (layer, 0, j)),
            pl.BlockSpec((tm, tn), lambda i, j: (i, j)),
        ],
        out_specs=pl.BlockSpec((tm, tn), lambda i, j: (i, j)),
        out_shape=jax.ShapeDtypeStruct((m, n), F32),
        compiler_params=_cparams("parallel", "arbitrary"),
        name="proj_residual",
    )(a, w, x)


def _na_mixer(x1, hmix, w_qkv, q_gain, k_gain, rpb, w_o, layer, batch, seq):
    d = x1.shape[1]
    n_heads = d // NA_HEAD_DIM
    gains = jnp.concatenate([jnp.tile(q_gain * (NA_HEAD_DIM ** -0.5), n_heads),
                             jnp.tile(k_gain, n_heads),
                             jnp.ones((d,), F32)]).reshape(1, 3 * d)
    qkv = _qkv(hmix, w_qkv, layer, gains)
    att = _attention(qkv, _bias_table(rpb), batch, seq, d)
    return _proj_residual(att, w_o, layer, x1)


def _gelu_exact(x):
    return 0.5 * x * (1.0 + lax.erf(x * (1.0 / math.sqrt(2.0))))


def _sgu_in_kernel(h_ref, w_ref, z_ref, ss_ref, *, n_u_blocks):
    j = pl.program_id(1)
    z = _gelu_exact(_dot(h_ref[...], w_ref[...]))
    z_ref[...] = z.astype(BF16)

    @pl.when(j == 0)
    def _():
        ss_ref[...] = jnp.zeros_like(ss_ref)

    @pl.when(j >= n_u_blocks)
    def _():
        ss_ref[...] += jnp.sum(z * z, axis=-1, keepdims=True)


def _sgu_in(h, w, layer):
    m, d = h.shape
    n = w.shape[2]
    tm = _pick(m, (1024, 512, 256, 128))
    tn = _pick(n // 2, (1024, 512, 256, 128))
    return pl.pallas_call(
        functools.partial(_sgu_in_kernel, n_u_blocks=n // 2 // tn),
        grid=(m // tm, n // tn),
        in_specs=[
            pl.BlockSpec((tm, d), lambda i, j: (i, 0)),
            pl.BlockSpec((None, d, tn), lambda i, j: (layer, 0, j)),
        ],
        out_specs=[
            pl.BlockSpec((tm, tn), lambda i, j: (i, j)),
            pl.BlockSpec((tm, LANES), lambda i, j: (i, 0)),
        ],
        out_shape=[jax.ShapeDtypeStruct((m, n), BF16), jax.ShapeDtypeStruct((m, LANES), F32)],
        compiler_params=_cparams("parallel", "arbitrary"),
        name="sgu_in",
    )(h, w)


def _sgu_out_kernel(u_ref, v_ref, ss_ref, vg_ref, ws_ref, bs_ref, wo_ref, x_ref, o_ref, t_ref,
                    *, d_sgu):
    tm = u_ref.shape[0]
    gd = d_sgu // SGU_GROUPS
    rinv = lax.rsqrt(ss_ref[:, 0:1] * (1.0 / d_sgu) + EPS)
    for g in range(SGU_GROUPS):
        sl = slice(g * gd, (g + 1) * gd)
        vn = (v_ref[:, sl].astype(F32) * rinv * vg_ref[:, sl]).astype(BF16)
        ws = ws_ref[g]
        bs = bs_ref[g]
        for c in range(tm // CHUNK):
            rs = slice(c * CHUNK, (c + 1) * CHUNK)
            mixed = _dot(ws, vn[rs]) + bs
            t_ref[rs, sl] = (u_ref[rs, sl].astype(F32) * mixed).astype(BF16)
    o_ref[...] = x_ref[...] + _dot(t_ref[...], wo_ref[...])


def _sgu_out(z, ss, v_gain, w_s, b_s, w_out, layer, x):
    m, d = x.shape
    d_sgu = z.shape[1] // 2
    tm = _pick(m, (256, 128))
    const2 = lambda i: (0, 0)
    const3 = lambda i: (0, 0, 0)
    return pl.pallas_call(
        functools.partial(_sgu_out_kernel, d_sgu=d_sgu),
        grid=(m // tm,),
        in_specs=[
            pl.BlockSpec((tm, d_sgu), lambda i: (i, 0)),
            pl.BlockSpec((tm, d_sgu), lambda i: (i, 1)),
            pl.BlockSpec((tm, LANES), lambda i: (i, 0)),
            pl.BlockSpec((1, d_sgu), const2),
            pl.BlockSpec((SGU_GROUPS, CHUNK, CHUNK), const3),
            pl.BlockSpec((SGU_GROUPS, CHUNK, 1), const3),
            pl.BlockSpec((None, d_sgu, d), lambda i: (layer, 0, 0), pipeline_mode=pl.Buffered(1)),
            pl.BlockSpec((tm, d), lambda i: (i, 0)),
        ],
        out_specs=pl.BlockSpec((tm, d), lambda i: (i, 0)),
        out_shape=jax.ShapeDtypeStruct((m, d), F32),
        scratch_shapes=[pltpu.VMEM((tm, d_sgu), BF16)],
        compiler_params=_cparams("parallel"),
        name="sgu_out",
    )(z, z, ss, v_gain.reshape(1, d_sgu), w_s, b_s.reshape(SGU_GROUPS, CHUNK, 1), w_out, x)


def _sgu_mixer(x1, hmix, w_in, v_gain, w_s, b_s, w_out, layer):
    z, ss = _sgu_in(hmix, w_in, layer)
    return _sgu_out(z, ss, v_gain, w_s, b_s, w_out, layer, x1)


def _pool_kernel(xp_ref, xc_ref, xn_ref, g_ref, w_ref, sc_ref, o_ref, *, seq):
    tm, d = xc_ref.shape
    gd = d // len(POOL_WINDOWS)
    gain = g_ref[...]
    xc = xc_ref[...]
    hc = _rms(xc, gain)
    hext = jnp.concatenate([_rms(xp_ref[...], gain), hc, _rms(xn_ref[...], gain)], axis=0)
    hext = hext.astype(BF16)
    pos0 = (pl.program_id(0) * tm) % seq
    t = lax.broadcasted_iota(jnp.int32, (tm, tm + 2 * POOL_HALO), 0)
    off = lax.broadcasted_iota(jnp.int32, (tm, tm + 2 * POOL_HALO), 1) - POOL_HALO - t
    key_pos = pos0 + t + off
    in_seq = (key_pos >= 0) & (key_pos < seq)
    tq = pos0 + lax.broadcasted_iota(jnp.int32, (tm, 1), 0)
    for g, w in enumerate(POOL_WINDOWS):
        sl = slice(g * gd, (g + 1) * gd)
        band = ((off >= -(w // 2)) & (off < w // 2) & in_seq).astype(BF16)
        cnt = (jnp.clip(tq + w // 2, 0, seq) - jnp.clip(tq - w // 2, 0, seq)).astype(F32)
        p = _dot(band, hext[:, sl]) / cnt - hc[:, sl]
        y = _dot(p.astype(BF16), w_ref[g])
        o_ref[:, sl] = xc[:, sl] + y * sc_ref[:, sl]


def _pool_mixer(x, gain, w_grp, scale, seq):
    m, d = x.shape
    tm = _pick(seq, (512, 256, 128))
    nb = tm // POOL_HALO
    last = m // POOL_HALO - 1
    ng, gd, _ = w_grp.shape
    return pl.pallas_call(
        functools.partial(_pool_kernel, seq=seq),
        grid=(m // tm,),
        in_specs=[
            pl.BlockSpec((POOL_HALO, d), lambda i: (jnp.maximum(i * nb - 1, 0), 0)),
            pl.BlockSpec((tm, d), lambda i: (i, 0)),
            pl.BlockSpec((POOL_HALO, d), lambda i: (jnp.minimum((i + 1) * nb, last), 0)),
            pl.BlockSpec((1, d), lambda i: (0, 0)),
            pl.BlockSpec((ng, gd, gd), lambda i: (0, 0, 0)),
            pl.BlockSpec((1, d), lambda i: (0, 0)),
        ],
        out_specs=pl.BlockSpec((tm, d), lambda i: (i, 0)),
        out_shape=jax.ShapeDtypeStruct((m, d), F32),
        compiler_params=_cparams("parallel"),
        name="pool_mixer",
    )(x, x, x, gain.reshape(1, d), w_grp, scale.reshape(1, d))


def kernel(x, ffn1_norm, ffn1_w_gate, ffn1_w_up, ffn1_w_down, mix_norm, ffn2_norm, ffn2_w_gate, ffn2_w_up, ffn2_w_down, out_norm, na_w_qkv, na_q_gain, na_k_gain, na_rpb, na_w_o, sgu_w_in, sgu_v_gain, sgu_w_s, sgu_b_s, sgu_w_out, pool_w, pool_scale):
    batch, seq, d = x.shape
    depth = ffn1_norm.shape[0]
    tf = _ffn_chunk(ffn1_w_down.shape[1])
    f1gu, f1d = _cast_gate_up(ffn1_w_gate, ffn1_w_up, tf), _cast_bf16(ffn1_w_down)
    f2gu, f2d = _cast_gate_up(ffn2_w_gate, ffn2_w_up, tf), _cast_bf16(ffn2_w_down)
    w_qkv, w_o = _cast_bf16(na_w_qkv), _cast_bf16(na_w_o)
    w_in, w_out = _cast_bf16(sgu_w_in), _cast_bf16(sgu_w_out)
    xs = x.reshape(batch * seq, d)
    for i in range(depth):
        kind, j = i % N_MIXERS, i // N_MIXERS
        emit_mix = kind != 2
        res = _ffn(xs, ffn1_norm[i], f1gu, f1d, i, out_norm[i], mix_norm[i],
                   post_norm=False, emit_mix=emit_mix)
        if kind == 0:
            x1, hmix = res
            x2 = _na_mixer(x1, hmix, w_qkv, na_q_gain[j], na_k_gain[j], na_rpb[j], w_o, j, batch, seq)
        elif kind == 1:
            x1, hmix = res
            x2 = _sgu_mixer(x1, hmix, w_in, sgu_v_gain[j], sgu_w_s[j].astype(BF16), sgu_b_s[j],
                            w_out, j)
        else:
            x2 = _pool_mixer(res, mix_norm[i], pool_w[j].astype(BF16), pool_scale[j], seq)
        xs = _ffn(x2, ffn2_norm[i], f2gu, f2d, i, out_norm[i], mix_norm[i],
                  post_norm=True, emit_mix=False)
    return xs.reshape(batch, seq, d)
```

```python
import functools
import math

import jax
import jax.numpy as jnp
import numpy as np
from jax import lax
from jax.experimental import pallas as pl
from jax.experimental.pallas import tpu as pltpu

EPS = 1e-6
N_MIXERS = 3
NA_HEAD_DIM = 32
GRID_W = 64
WIN_H = 8
WIN_W = 16
CHUNK = 128
SGU_GROUPS = 16
POOL_WINDOWS = (2, 4, 8, 16)
POOL_HALO = 64

LANES = 128
MXU_N = 256
HEADS_PER_GROUP = MXU_N // NA_HEAD_DIM
VMEM_LIMIT = 61440 * 1024
CAST_BLOCK_BYTES = 6 * 1024 * 1024
MASK_VALUE = -1e30

BF16 = jnp.bfloat16
F32 = jnp.float32


def _cparams(*sem):
    return pltpu.CompilerParams(dimension_semantics=sem, vmem_limit_bytes=VMEM_LIMIT)


def _rms(x, g):
    return x * lax.rsqrt(jnp.mean(x * x, axis=-1, keepdims=True) + EPS) * g


def _dot(a, b):
    return jnp.dot(a, b, preferred_element_type=F32)


def _aligned(i, m):
    return i * m if isinstance(i, int) else pl.multiple_of(i * m, m)


def _pick(n, prefs):
    for p in prefs:
        if n % p == 0:
            return p
    return n


def _cast_kernel(w_ref, o_ref):
    o_ref[...] = w_ref[...].astype(BF16)


def _cast_bf16(w):
    l, r, c = w.shape
    if l == 0:
        return w.astype(BF16)
    tr = r
    while tr * c * 4 > CAST_BLOCK_BYTES and tr % 2 == 0 and (tr // 2) % 16 == 0:
        tr //= 2
    return pl.pallas_call(
        _cast_kernel,
        grid=(l, r // tr),
        in_specs=[pl.BlockSpec((1, tr, c), lambda a, b: (a, b, 0))],
        out_specs=pl.BlockSpec((1, tr, c), lambda a, b: (a, b, 0)),
        out_shape=jax.ShapeDtypeStruct(w.shape, BF16),
        compiler_params=_cparams("parallel", "parallel"),
        name="cast_bf16",
    )(w)


def _cast_layer(w, layer):
    _, r, c = w.shape
    tr = r
    while tr * c * 4 > CAST_BLOCK_BYTES and tr % 2 == 0 and (tr // 2) % 16 == 0:
        tr //= 2
    return pl.pallas_call(
        _cast_kernel,
        grid=(r // tr,),
        in_specs=[pl.BlockSpec((None, tr, c), lambda b: (layer, b, 0))],
        out_specs=pl.BlockSpec((tr, c), lambda b: (b, 0)),
        out_shape=jax.ShapeDtypeStruct((r, c), BF16),
        compiler_params=_cparams("parallel"),
        name="cast_layer",
    )(w)


def _cast_pair(a_ref, b_ref, o_ref):
    n = a_ref.shape[-1]
    o_ref[:, :n] = a_ref[...].astype(BF16)
    o_ref[:, n:] = b_ref[...].astype(BF16)


def _cast_gate_up(wg, wu, layer, tf):
    _, d, f = wg.shape
    tr = _pick(d, (1024, 512, 256, 128))
    return pl.pallas_call(
        _cast_pair,
        grid=(f // tf, d // tr),
        in_specs=[pl.BlockSpec((None, tr, tf), lambda j, r: (layer, r, j)),
                  pl.BlockSpec((None, tr, tf), lambda j, r: (layer, r, j))],
        out_specs=pl.BlockSpec((None, tr, 2 * tf), lambda j, r: (j, r, 0)),
        out_shape=jax.ShapeDtypeStruct((f // tf, d, 2 * tf), BF16),
        compiler_params=_cparams("parallel", "parallel"),
        name="cast_gate_up",
    )(wg, wu)


def _ffn_kernel(x_ref, g_ref, wgu_ref, wd_ref, og_ref, mg_ref, *rest,
                post_norm, emit_mix, cast_next, n_chunks):
    rest = list(rest)
    next_f32 = [rest.pop(0) for _ in range(3)] if cast_next else None
    y_ref = rest.pop(0)
    if emit_mix:
        hm_ref = rest.pop(0)
        h_ref = hm_ref
    next_bf16 = [rest.pop(0) for _ in range(2)] if cast_next else None
    if not emit_mix:
        h_ref = rest.pop(0)
    j = pl.program_id(1)

    def cast_next_weights():
        wg_n, wu_n, wd_n = next_f32
        wgu_o, wd_o = next_bf16
        _cast_pair(wg_n, wu_n, wgu_o)
        wd_o[...] = wd_n[...].astype(BF16)

    def chunk():
        tf = wd_ref.shape[0]
        gu = _dot(h_ref[...], wgu_ref[...])
        gate, up = gu[:, :tf], gu[:, tf:]
        act = (gate * jax.nn.sigmoid(gate) * up * 0.5).astype(BF16)
        return _dot(act, wd_ref[...])

    def step(first, last):
        if cast_next:
            cast_next_weights()
        if first:
            h_ref[...] = _rms(x_ref[...], g_ref[...]).astype(BF16)
        y = (x_ref[...] if first else y_ref[...]) + chunk()
        if last and post_norm:
            y = _rms(y, og_ref[...])
        y_ref[...] = y
        if last and emit_mix:
            hm_ref[...] = _rms(y, mg_ref[...]).astype(BF16)

    if n_chunks == 1:
        step(True, True)
    else:
        pl.when(j == 0)(lambda: step(True, False))
        pl.when(j == n_chunks - 1)(lambda: step(False, True))
        if n_chunks > 2:
            pl.when((j > 0) & (j < n_chunks - 1))(lambda: step(False, False))


def _ffn_chunk(f):
    return _pick(f, (512, 256, 128))


def _ffn_tiles(m, f):
    return _pick(m, (1024, 512, 256, 128)), _ffn_chunk(f)


def _ffn_can_cast_next(m, d, f):
    tm, tf = _ffn_tiles(m, f)
    n_i, n_j = m // tm, f // tf
    return d % (16 * n_i) == 0 and f % (16 * n_i * n_j) == 0


def _ffn(x, gain, wgu, wd, out_gain, mix_gain, *, post_norm, emit_mix, next_weights=None):
    m, d = x.shape
    f = wd.shape[0]
    tm, tf = _ffn_tiles(m, f)
    n_i, n_j = m // tm, f // tf
    row = lambda i, j: (i, 0)
    vec = lambda i, j: (0, 0)
    cast_next = next_weights is not None
    in_specs = [
        pl.BlockSpec((tm, d), row),
        pl.BlockSpec((1, d), vec),
        pl.BlockSpec((None, d, 2 * tf), lambda i, j: (j, 0, 0)),
        pl.BlockSpec((tf, d), lambda i, j: (j, 0)),
        pl.BlockSpec((1, d), vec),
        pl.BlockSpec((1, d), vec),
    ]
    args = [x, gain.reshape(1, d), wgu, wd, out_gain.reshape(1, d), mix_gain.reshape(1, d)]
    out_shape = [jax.ShapeDtypeStruct((m, d), F32)]
    out_specs = [pl.BlockSpec((tm, d), row)]
    if emit_mix:
        out_shape.append(jax.ShapeDtypeStruct((m, d), BF16))
        out_specs.append(pl.BlockSpec((tm, d), row))
    if cast_next:
        wg_n, wu_n, wd_n, layer = next_weights
        rg, rd = d // n_i, f // (n_i * n_j)
        in_specs += [
            pl.BlockSpec((None, rg, tf), lambda i, j: (layer, i, j)),
            pl.BlockSpec((None, rg, tf), lambda i, j: (layer, i, j)),
            pl.BlockSpec((None, rd, d), lambda i, j: (layer, i * n_j + j, 0)),
        ]
        args += [wg_n, wu_n, wd_n]
        out_shape += [jax.ShapeDtypeStruct((n_j, d, 2 * tf), BF16), jax.ShapeDtypeStruct((f, d), BF16)]
        out_specs += [pl.BlockSpec((None, rg, 2 * tf), lambda i, j: (j, i, 0)),
                      pl.BlockSpec((rd, d), lambda i, j: (i * n_j + j, 0))]
    res = pl.pallas_call(
        functools.partial(_ffn_kernel, post_norm=post_norm, emit_mix=emit_mix, cast_next=cast_next,
                          n_chunks=n_j),
        grid=(n_i, n_j),
        in_specs=in_specs,
        out_specs=out_specs,
        out_shape=out_shape,
        scratch_shapes=[] if emit_mix else [pltpu.VMEM((tm, d), BF16)],
        compiler_params=_cparams("parallel", "arbitrary"),
        name="ffn",
    )(*args)
    y = res[0]
    hm = res[1] if emit_mix else None
    nxt = tuple(res[-2:]) if cast_next else None
    return y, hm, nxt


def _qkv_kernel(h_ref, w_ref, seg_ref, gain_ref, o_ref, *, n_norm_blocks):
    j = pl.program_id(1)

    @pl.when(j < n_norm_blocks)
    def _():
        seg = seg_ref[...]
        h = h_ref[...]
        n_sub = w_ref.shape[1] // MXU_N
        cols = [slice(c * MXU_N, (c + 1) * MXU_N) for c in range(n_sub)]

        def normalise(a, sl):
            sq = a * a
            hi = sq.astype(BF16)
            lo = (sq - hi.astype(F32)).astype(BF16)
            ss = _dot(hi, seg) + _dot(lo, seg)
            y = a * lax.rsqrt(ss * (1.0 / NA_HEAD_DIM) + EPS) * gain_ref[:, sl]
            o_ref[:, sl] = y.astype(BF16)

        a_prev = _dot(h, w_ref[:, cols[0]])
        for c in range(1, n_sub):
            a_next = _dot(h, w_ref[:, cols[c]])
            normalise(a_prev, cols[c - 1])
            a_prev = a_next
        normalise(a_prev, cols[n_sub - 1])

    @pl.when(j >= n_norm_blocks)
    def _():
        o_ref[...] = _dot(h_ref[...], w_ref[...]).astype(BF16)


def _qkv(h, w, layer, gains):
    m, d = h.shape
    n = w.shape[2]
    tm = _pick(m, (1024, 512, 256, 128))
    tn = _pick(d, (1024, 512, 256))
    head = np.arange(MXU_N) // NA_HEAD_DIM
    seg = jnp.asarray(head[:, None] == head[None, :], BF16)
    return pl.pallas_call(
        functools.partial(_qkv_kernel, n_norm_blocks=2 * d // tn),
        grid=(m // tm, n // tn),
        in_specs=[
            pl.BlockSpec((tm, d), lambda i, j: (i, 0)),
            pl.BlockSpec((None, d, tn), lambda i, j: (layer, 0, j)),
            pl.BlockSpec((MXU_N, MXU_N), lambda i, j: (0, 0)),
            pl.BlockSpec((1, tn), lambda i, j: (0, j)),
        ],
        out_specs=pl.BlockSpec((tm, tn), lambda i, j: (i, j)),
        out_shape=jax.ShapeDtypeStruct((m, n), BF16),
        compiler_params=_cparams("parallel", "arbitrary"),
        name="na_qkv",
    )(h, w, seg, gains)


KEY_BLK = 32
QUERY_BLOCKS = ((0, 24, 0), (24, 16, 16), (40, 24, 32))
ATTN_ROWS = HEADS_PER_GROUP * GRID_W
ROWS_PER_ITER = 8


def _check_query_blocks():
    cols = np.arange(GRID_W)
    start = np.clip(cols - WIN_W // 2, 0, GRID_W - WIN_W)
    covered = 0
    for c0q, nq, c0k in QUERY_BLOCKS:
        assert c0q == covered and c0k % 16 == 0 and nq % 8 == 0
        assert (start[c0q:c0q + nq] >= c0k).all() and (start[c0q:c0q + nq] + WIN_W <= c0k + KEY_BLK).all()
        covered += nq
    assert covered == GRID_W


_check_query_blocks()


def _attn_rows(items, q_ref, k_ref, v_ref, t_ref, qm_ref, o_ref, *, rows):
    nt = (((1,), (1,)), ((), ()))

    def window(ref, rs, c0k):
        parts = []
        for i in range(WIN_H):
            start = rs * GRID_W + (i * GRID_W + c0k)
            if not isinstance(start, int):
                start = pl.multiple_of(start, 16)
            parts.append(ref[0, pl.ds(start, KEY_BLK), :])
        return jnp.concatenate(parts, axis=0)

    tiles = []
    for r_local, row, d in items:
        if isinstance(row, int):
            rs = min(max(row - WIN_H // 2, 0), rows - WIN_H)
        else:
            rs = jnp.clip(row - WIN_H // 2, 0, rows - WIN_H)
        qf = q_ref[0, pl.ds(_aligned(r_local, GRID_W), GRID_W), :].astype(F32)
        row_off = 0
        for c0q, nq, c0k in QUERY_BLOCKS:
            nrow = nq * HEADS_PER_GROUP
            rep = 1 if nq % 16 == 0 else 2
            qb = jnp.concatenate([qf[c0q:c0q + nq]] * rep, axis=0).astype(BF16)
            qs = jnp.tile(qb, (HEADS_PER_GROUP // rep, 1)) * qm_ref[row_off:row_off + nrow, :]
            s = lax.dot_general(qs, window(k_ref, rs, c0k), nt, preferred_element_type=F32)
            tiles.append((s, rs, d, row_off, nq, c0k))
            row_off += nrow

    probs = []
    for s, rs, d, row_off, nq, c0k in tiles:
        nrow = nq * HEADS_PER_GROUP
        bias_off = (WIN_H - d) * KEY_BLK
        s = s + t_ref[row_off:row_off + nrow, bias_off:bias_off + WIN_H * KEY_BLK]
        m = jnp.max(s, axis=-1, keepdims=True)
        p = jnp.exp(s - m)
        l = jnp.sum(p, axis=-1, keepdims=True)
        probs.append((p * (1.0 / l)).astype(BF16))

    outs = []
    for pn, (s, rs, d, row_off, nq, c0k) in zip(probs, tiles):
        o = _dot(pn, window(v_ref, rs, c0k))
        lane_head = lax.broadcasted_iota(jnp.int32, (nq, MXU_N), 1) // NA_HEAD_DIM
        acc = o[(HEADS_PER_GROUP - 1) * nq:]
        for hh in range(HEADS_PER_GROUP - 2, -1, -1):
            acc = jnp.where(lane_head == hh, o[hh * nq:(hh + 1) * nq], acc)
        outs.append(acc)

    nb = len(QUERY_BLOCKS)
    for n, (r_local, row, d) in enumerate(items):
        out = jnp.concatenate(outs[n * nb:(n + 1) * nb], axis=0)
        o_ref[0, pl.ds(_aligned(r_local, GRID_W), GRID_W), :] = out.astype(BF16)


def _attn_kernel(q_ref, k_ref, v_ref, t_ref, qm_ref, o_ref, *, rows, rows_per_step):
    half = WIN_H // 2
    rc = pl.program_id(2)
    n_rc = pl.num_programs(2)
    row0 = rc * rows_per_step
    body = functools.partial(_attn_rows, q_ref=q_ref, k_ref=k_ref, v_ref=v_ref, t_ref=t_ref,
                             qm_ref=qm_ref, o_ref=o_ref, rows=rows)
    edge = -(-half // ROWS_PER_ITER) * ROWS_PER_ITER
    lo = jnp.where(rc == 0, edge, 0)
    hi = jnp.where(rc == n_rc - 1, rows_per_step - edge, rows_per_step)

    def interior(it, carry):
        base = it * ROWS_PER_ITER
        body([(base + u, row0 + base + u, half) for u in range(ROWS_PER_ITER)])
        return carry

    lax.fori_loop(lo // ROWS_PER_ITER, hi // ROWS_PER_ITER, interior, 0)

    def window_offset(r):
        return r - min(max(r - half, 0), rows - WIN_H)

    def static_rows(first, local_shift):
        for r0 in range(first, first + edge, ROWS_PER_ITER):
            body([(r - local_shift, r, window_offset(r)) for r in range(r0, r0 + ROWS_PER_ITER)])

    @pl.when(rc == 0)
    def _():
        static_rows(0, 0)

    @pl.when(rc == n_rc - 1)
    def _():
        static_rows(rows - edge, rows - rows_per_step)


def _score_row_layout():
    h, c, c0 = [], [], []
    for c0q, nq, c0k in QUERY_BLOCKS:
        for hh in range(HEADS_PER_GROUP):
            h += [hh] * nq
            c += list(range(c0q, c0q + nq))
            c0 += [c0k] * nq
    return np.array(h), np.array(c), np.array(c0)


def _bias_table(rpb):
    n_dr = 2 * WIN_H - 1
    n_hg = rpb.shape[0] // HEADS_PER_GROUP
    parts = []
    for c0q, nq, c0k in QUERY_BLOCKS:
        c = np.arange(c0q, c0q + nq)
        start = np.clip(c - WIN_W // 2, 0, GRID_W - WIN_W)
        kc = c0k + np.arange(KEY_BLK)[None, :] + 0 * c[:, None]
        inside = (kc >= start[:, None]) & (kc < start[:, None] + WIN_W)
        dc = np.clip(kc - c[:, None] + WIN_W - 1, 0, 2 * WIN_W - 2)
        t = jnp.where(inside[None, None], rpb[:, :, dc], MASK_VALUE)
        t = jnp.transpose(t, (0, 2, 1, 3))
        parts.append(t.reshape(n_hg, HEADS_PER_GROUP * nq, n_dr * KEY_BLK))
    t = jnp.concatenate(parts, axis=1)
    return jnp.pad(t, ((0, 0), (0, 0), (KEY_BLK, 0)), constant_values=MASK_VALUE)


def _head_mask():
    h, _, _ = _score_row_layout()
    return jnp.asarray(h[:, None] == (np.arange(MXU_N) // NA_HEAD_DIM)[None, :], BF16)


def _attention(qkv, table, batch, seq, d):
    rows = seq // GRID_W
    rps = _pick(rows, (32, 16, 8))
    assert rps % ROWS_PER_ITER == 0 and rows >= 2 * WIN_H and rps >= 2 * ROWS_PER_ITER
    qkv3 = qkv.reshape(batch, seq, 3 * d)
    n_hg = d // MXU_N
    tw = table.shape[-1]
    out = pl.pallas_call(
        functools.partial(_attn_kernel, rows=rows, rows_per_step=rps),
        grid=(batch, n_hg, rows // rps),
        in_specs=[
            pl.BlockSpec((1, rps * GRID_W, MXU_N), lambda b, g, r: (b, r, g)),
            pl.BlockSpec((1, seq, MXU_N), lambda b, g, r: (b, 0, n_hg + g)),
            pl.BlockSpec((1, seq, MXU_N), lambda b, g, r: (b, 0, 2 * n_hg + g)),
            pl.BlockSpec((None, ATTN_ROWS, tw), lambda b, g, r: (g, 0, 0)),
            pl.BlockSpec((ATTN_ROWS, MXU_N), lambda b, g, r: (0, 0)),
        ],
        out_specs=pl.BlockSpec((1, rps * GRID_W, MXU_N), lambda b, g, r: (b, r, g)),
        out_shape=jax.ShapeDtypeStruct((batch, seq, d), BF16),
        compiler_params=_cparams("parallel", "parallel", "arbitrary"),
        name="na_attn",
    )(qkv3, qkv3, qkv3, table, _head_mask())
    return out.reshape(batch * seq, d)


def _proj_kernel(a_ref, w_ref, x_ref, o_ref):
    o_ref[...] = x_ref[...] + _dot(a_ref[...], w_ref[...])


def _proj_residual(a, w, layer, x):
    m, k = a.shape
    n = w.shape[2]
    tm = _pick(m, (512, 256, 128))
    tn = n
    return pl.pallas_call(
        _proj_kernel,
        grid=(m // tm, n // tn),
        in_specs=[
            pl.BlockSpec((tm, k), lambda i, j: (i, 0)),
            pl.BlockSpec((None, k, tn), lambda i, j: (layer, 0, j)),
            pl.BlockSpec((tm, tn), lambda i, j: (i, j)),
        ],
        out_specs=pl.BlockSpec((tm, tn), lambda i, j: (i, j)),
        out_shape=jax.ShapeDtypeStruct((m, n), F32),
        compiler_params=_cparams("parallel", "arbitrary"),
        name="proj_residual",
    )(a, w, x)


def _na_mixer(x1, hmix, w_qkv, q_gain, k_gain, rpb, w_o, layer, batch, seq):
    d = x1.shape[1]
    n_heads = d // NA_HEAD_DIM
    gains = jnp.concatenate([jnp.tile(q_gain * (NA_HEAD_DIM ** -0.5), n_heads),
                             jnp.tile(k_gain, n_heads),
                             jnp.ones((d,), F32)]).reshape(1, 3 * d)
    qkv = _qkv(hmix, w_qkv, layer, gains)
    att = _attention(qkv, _bias_table(rpb), batch, seq, d)
    return _proj_residual(att, w_o, layer, x1)


def _gelu_exact(x):
    return 0.5 * x * (1.0 + lax.erf(x * (1.0 / math.sqrt(2.0))))


def _sgu_in_kernel(h_ref, w_ref, z_ref, ss_ref, *, n_u_blocks):
    j = pl.program_id(1)
    z = _gelu_exact(_dot(h_ref[...], w_ref[...]))
    z_ref[...] = z.astype(BF16)

    @pl.when(j == 0)
    def _():
        ss_ref[...] = jnp.zeros_like(ss_ref)

    @pl.when(j >= n_u_blocks)
    def _():
        ss_ref[...] += jnp.sum(z * z, axis=-1, keepdims=True)


def _sgu_in(h, w, layer):
    m, d = h.shape
    n = w.shape[2]
    tm = _pick(m, (1024, 512, 256, 128))
    tn = _pick(n // 2, (1024, 512, 256, 128))
    return pl.pallas_call(
        functools.partial(_sgu_in_kernel, n_u_blocks=n // 2 // tn),
        grid=(m // tm, n // tn),
        in_specs=[
            pl.BlockSpec((tm, d), lambda i, j: (i, 0)),
            pl.BlockSpec((None, d, tn), lambda i, j: (layer, 0, j)),
        ],
        out_specs=[
            pl.BlockSpec((tm, tn), lambda i, j: (i, j)),
            pl.BlockSpec((tm, LANES), lambda i, j: (i, 0)),
        ],
        out_shape=[jax.ShapeDtypeStruct((m, n), BF16), jax.ShapeDtypeStruct((m, LANES), F32)],
        compiler_params=_cparams("parallel", "arbitrary"),
        name="sgu_in",
    )(h, w)


def _sgu_out_kernel(u_ref, v_ref, ss_ref, vg_ref, ws_ref, bs_ref, wo_ref, x_ref, o_ref, t_ref,
                    *, d_sgu):
    tm = u_ref.shape[0]
    gd = d_sgu // SGU_GROUPS
    rinv = lax.rsqrt(ss_ref[:, 0:1] * (1.0 / d_sgu) + EPS)
    for g in range(SGU_GROUPS):
        sl = slice(g * gd, (g + 1) * gd)
        vn = (v_ref[:, sl].astype(F32) * rinv * vg_ref[:, sl]).astype(BF16)
        ws = ws_ref[g]
        bs = bs_ref[g]
        for c in range(tm // CHUNK):
            rs = slice(c * CHUNK, (c + 1) * CHUNK)
            mixed = _dot(ws, vn[rs]) + bs
            t_ref[rs, sl] = (u_ref[rs, sl].astype(F32) * mixed).astype(BF16)
    o_ref[...] = x_ref[...] + _dot(t_ref[...], wo_ref[...])


def _sgu_out(z, ss, v_gain, w_s, b_s, w_out, layer, x):
    m, d = x.shape
    d_sgu = z.shape[1] // 2
    tm = _pick(m, (256, 128))
    const2 = lambda i: (0, 0)
    const3 = lambda i: (0, 0, 0)
    return pl.pallas_call(
        functools.partial(_sgu_out_kernel, d_sgu=d_sgu),
        grid=(m // tm,),
        in_specs=[
            pl.BlockSpec((tm, d_sgu), lambda i: (i, 0)),
            pl.BlockSpec((tm, d_sgu), lambda i: (i, 1)),
            pl.BlockSpec((tm, LANES), lambda i: (i, 0)),
            pl.BlockSpec((1, d_sgu), const2),
            pl.BlockSpec((SGU_GROUPS, CHUNK, CHUNK), const3),
            pl.BlockSpec((SGU_GROUPS, CHUNK, 1), const3),
            pl.BlockSpec((None, d_sgu, d), lambda i: (layer, 0, 0), pipeline_mode=pl.Buffered(1)),
            pl.BlockSpec((tm, d), lambda i: (i, 0)),
        ],
        out_specs=pl.BlockSpec((tm, d), lambda i: (i, 0)),
        out_shape=jax.ShapeDtypeStruct((m, d), F32),
        scratch_shapes=[pltpu.VMEM((tm, d_sgu), BF16)],
        compiler_params=_cparams("parallel"),
        name="sgu_out",
    )(z, z, ss, v_gain.reshape(1, d_sgu), w_s, b_s.reshape(SGU_GROUPS, CHUNK, 1), w_out, x)


def _sgu_mixer(x1, hmix, w_in, v_gain, w_s, b_s, w_out, layer):
    z, ss = _sgu_in(hmix, w_in, layer)
    return _sgu_out(z, ss, v_gain, w_s, b_s, w_out, layer, x1)


def _pool_kernel(xp_ref, xc_ref, xn_ref, g_ref, w_ref, sc_ref, o_ref, *, seq):
    tm, d = xc_ref.shape
    gd = d // len(POOL_WINDOWS)
    gain = g_ref[...]
    xc = xc_ref[...]
    hc = _rms(xc, gain)
    hext = jnp.concatenate([_rms(xp_ref[...], gain), hc, _rms(xn_ref[...], gain)], axis=0)
    hext = hext.astype(BF16)
    pos0 = (pl.program_id(0) * tm) % seq
    t = lax.broadcasted_iota(jnp.int32, (tm, tm + 2 * POOL_HALO), 0)
    off = lax.broadcasted_iota(jnp.int32, (tm, tm + 2 * POOL_HALO), 1) - POOL_HALO - t
    key_pos = pos0 + t + off
    in_seq = (key_pos >= 0) & (key_pos < seq)
    tq = pos0 + lax.broadcasted_iota(jnp.int32, (tm, 1), 0)
    for g, w in enumerate(POOL_WINDOWS):
        sl = slice(g * gd, (g + 1) * gd)
        band = ((off >= -(w // 2)) & (off < w // 2) & in_seq).astype(BF16)
        cnt = (jnp.clip(tq + w // 2, 0, seq) - jnp.clip(tq - w // 2, 0, seq)).astype(F32)
        p = _dot(band, hext[:, sl]) / cnt - hc[:, sl]
        y = _dot(p.astype(BF16), w_ref[g])
        o_ref[:, sl] = xc[:, sl] + y * sc_ref[:, sl]


def _pool_mixer(x, gain, w_grp, scale, seq):
    m, d = x.shape
    tm = _pick(seq, (512, 256, 128))
    nb = tm // POOL_HALO
    last = m // POOL_HALO - 1
    ng, gd, _ = w_grp.shape
    return pl.pallas_call(
        functools.partial(_pool_kernel, seq=seq),
        grid=(m // tm,),
        in_specs=[
            pl.BlockSpec((POOL_HALO, d), lambda i: (jnp.maximum(i * nb - 1, 0), 0)),
            pl.BlockSpec((tm, d), lambda i: (i, 0)),
            pl.BlockSpec((POOL_HALO, d), lambda i: (jnp.minimum((i + 1) * nb, last), 0)),
            pl.BlockSpec((1, d), lambda i: (0, 0)),
            pl.BlockSpec((ng, gd, gd), lambda i: (0, 0, 0)),
            pl.BlockSpec((1, d), lambda i: (0, 0)),
        ],
        out_specs=pl.BlockSpec((tm, d), lambda i: (i, 0)),
        out_shape=jax.ShapeDtypeStruct((m, d), F32),
        compiler_params=_cparams("parallel"),
        name="pool_mixer",
    )(x, x, x, gain.reshape(1, d), w_grp, scale.reshape(1, d))


def kernel(x, ffn1_norm, ffn1_w_gate, ffn1_w_up, ffn1_w_down, mix_norm, ffn2_norm, ffn2_w_gate, ffn2_w_up, ffn2_w_down, out_norm, na_w_qkv, na_q_gain, na_k_gain, na_rpb, na_w_o, sgu_w_in, sgu_v_gain, sgu_w_s, sgu_b_s, sgu_w_out, pool_w, pool_scale):
    batch, seq, d = x.shape
    depth = ffn1_norm.shape[0]
    f = ffn1_w_down.shape[1]
    tf = _ffn_chunk(f)
    w_qkv, w_o = _cast_bf16(na_w_qkv), _cast_bf16(na_w_o)
    w_in, w_out = _cast_bf16(sgu_w_in), _cast_bf16(sgu_w_out)
    xs = x.reshape(batch * seq, d)

    stacks = [(ffn1_w_gate, ffn1_w_up, ffn1_w_down), (ffn2_w_gate, ffn2_w_up, ffn2_w_down)]
    order = [(stacks[k], i) for i in range(depth) for k in range(2)]
    fuse_cast = _ffn_can_cast_next(batch * seq, d, f)

    def standalone_cast(n):
        (wg, wu, wd), layer = order[n]
        return _cast_gate_up(wg, wu, layer, tf), _cast_layer(wd, layer)

    def run_ffn(n, xin, gain, cur, **kw):
        nxt = None
        if fuse_cast and n + 1 < len(order):
            nxt = (*order[n + 1][0], order[n + 1][1])
        y, hm, cast = _ffn(xin, gain, cur[0], cur[1], out_norm[n // 2], mix_norm[n // 2],
                           next_weights=nxt, **kw)
        if cast is None and n + 1 < len(order):
            cast = standalone_cast(n + 1)
        return y, hm, cast

    cur = standalone_cast(0)
    for i in range(depth):
        kind, j = i % N_MIXERS, i // N_MIXERS
        x1, hmix, cur = run_ffn(2 * i, xs, ffn1_norm[i], cur, post_norm=False, emit_mix=kind != 2)
        if kind == 0:
            x2 = _na_mixer(x1, hmix, w_qkv, na_q_gain[j], na_k_gain[j], na_rpb[j], w_o, j, batch, seq)
        elif kind == 1:
            x2 = _sgu_mixer(x1, hmix, w_in, sgu_v_gain[j], sgu_w_s[j].astype(BF16), sgu_b_s[j],
                            w_out, j)
        else:
            x2 = _pool_mixer(x1, mix_norm[i], pool_w[j].astype(BF16), pool_scale[j], seq)
        xs, _, cur = run_ffn(2 * i + 1, x2, ffn2_norm[i], cur, post_norm=True, emit_mix=False)
    return xs.reshape(batch, seq, d)
```

```python
import functools
import math

import jax
import jax.numpy as jnp
import numpy as np
from jax import lax
from jax.experimental import pallas as pl
from jax.experimental.pallas import tpu as pltpu

EPS = 1e-6
N_MIXERS = 3
NA_HEAD_DIM = 32
GRID_W = 64
WIN_H = 8
WIN_W = 16
CHUNK = 128
SGU_GROUPS = 16
POOL_WINDOWS = (2, 4, 8, 16)
POOL_HALO = 64

LANES = 128
MXU_N = 256
HEADS_PER_GROUP = MXU_N // NA_HEAD_DIM
VMEM_LIMIT = 61440 * 1024
CAST_BLOCK_BYTES = 6 * 1024 * 1024
MASK_VALUE = -1e30

BF16 = jnp.bfloat16
F32 = jnp.float32


def _cparams(*sem):
    return pltpu.CompilerParams(dimension_semantics=sem, vmem_limit_bytes=VMEM_LIMIT)


def _rms(x, g):
    return x * lax.rsqrt(jnp.mean(x * x, axis=-1, keepdims=True) + EPS) * g


def _dot(a, b):
    return jnp.dot(a, b, preferred_element_type=F32)


def _aligned(i, m):
    return i * m if isinstance(i, int) else pl.multiple_of(i * m, m)


def _pick(n, prefs):
    for p in prefs:
        if n % p == 0:
            return p
    return n


def _cast_kernel(w_ref, o_ref):
    o_ref[...] = w_ref[...].astype(BF16)


def _cast_bf16(w):
    l, r, c = w.shape
    if l == 0:
        return w.astype(BF16)
    tr = r
    while tr * c * 4 > CAST_BLOCK_BYTES and tr % 2 == 0 and (tr // 2) % 16 == 0:
        tr //= 2
    return pl.pallas_call(
        _cast_kernel,
        grid=(l, r // tr),
        in_specs=[pl.BlockSpec((1, tr, c), lambda a, b: (a, b, 0))],
        out_specs=pl.BlockSpec((1, tr, c), lambda a, b: (a, b, 0)),
        out_shape=jax.ShapeDtypeStruct(w.shape, BF16),
        compiler_params=_cparams("parallel", "parallel"),
        name="cast_bf16",
    )(w)


def _cast_layer(w, layer):
    _, r, c = w.shape
    tr = r
    while tr * c * 4 > CAST_BLOCK_BYTES and tr % 2 == 0 and (tr // 2) % 16 == 0:
        tr //= 2
    return pl.pallas_call(
        _cast_kernel,
        grid=(r // tr,),
        in_specs=[pl.BlockSpec((None, tr, c), lambda b: (layer, b, 0))],
        out_specs=pl.BlockSpec((tr, c), lambda b: (b, 0)),
        out_shape=jax.ShapeDtypeStruct((r, c), BF16),
        compiler_params=_cparams("parallel"),
        name="cast_layer",
    )(w)


def _cast_pair(a_ref, b_ref, o_ref):
    n = a_ref.shape[-1]
    o_ref[:, :n] = a_ref[...].astype(BF16)
    o_ref[:, n:] = b_ref[...].astype(BF16)


def _cast_gate_up(wg, wu, layer, tf):
    _, d, f = wg.shape
    tr = _pick(d, (1024, 512, 256, 128))
    return pl.pallas_call(
        _cast_pair,
        grid=(f // tf, d // tr),
        in_specs=[pl.BlockSpec((None, tr, tf), lambda j, r: (layer, r, j)),
                  pl.BlockSpec((None, tr, tf), lambda j, r: (layer, r, j))],
        out_specs=pl.BlockSpec((None, tr, 2 * tf), lambda j, r: (j, r, 0)),
        out_shape=jax.ShapeDtypeStruct((f // tf, d, 2 * tf), BF16),
        compiler_params=_cparams("parallel", "parallel"),
        name="cast_gate_up",
    )(wg, wu)


def _ffn_kernel(x_ref, g_ref, wgu_ref, wd_ref, og_ref, mg_ref, *rest,
                post_norm, emit_mix, cast_next, n_chunks):
    rest = list(rest)
    next_f32 = [rest.pop(0) for _ in range(3)] if cast_next else None
    y_ref = rest.pop(0)
    if emit_mix:
        hm_ref = rest.pop(0)
        h_ref = hm_ref
    next_bf16 = [rest.pop(0) for _ in range(2)] if cast_next else None
    if not emit_mix:
        h_ref = rest.pop(0)
    j = pl.program_id(1)

    def cast_next_weights():
        wg_n, wu_n, wd_n = next_f32
        wgu_o, wd_o = next_bf16
        _cast_pair(wg_n, wu_n, wgu_o)
        wd_o[...] = wd_n[...].astype(BF16)

    def chunk():
        tf = wd_ref.shape[0]
        gu = _dot(h_ref[...], wgu_ref[...])
        gate, up = gu[:, :tf], gu[:, tf:]
        act = (gate * jax.nn.sigmoid(gate) * up * 0.5).astype(BF16)
        return _dot(act, wd_ref[...])

    def step(first, last):
        if cast_next:
            cast_next_weights()
        if first:
            h_ref[...] = _rms(x_ref[...], g_ref[...]).astype(BF16)
        y = (x_ref[...] if first else y_ref[...]) + chunk()
        if last and post_norm:
            y = _rms(y, og_ref[...])
        y_ref[...] = y
        if last and emit_mix:
            hm_ref[...] = _rms(y, mg_ref[...]).astype(BF16)

    if n_chunks == 1:
        step(True, True)
    else:
        pl.when(j == 0)(lambda: step(True, False))
        pl.when(j == n_chunks - 1)(lambda: step(False, True))
        if n_chunks > 2:
            pl.when((j > 0) & (j < n_chunks - 1))(lambda: step(False, False))


def _ffn_chunk(f):
    return _pick(f, (512, 256, 128))


def _ffn_tiles(m, f):
    return _pick(m, (1024, 512, 256, 128)), _ffn_chunk(f)


def _ffn_can_cast_next(m, d, f):
    tm, tf = _ffn_tiles(m, f)
    n_i, n_j = m // tm, f // tf
    return d % (16 * n_i) == 0 and f % (16 * n_i * n_j) == 0


def _ffn(x, gain, wgu, wd, out_gain, mix_gain, *, post_norm, emit_mix, next_weights=None):
    m, d = x.shape
    f = wd.shape[0]
    tm, tf = _ffn_tiles(m, f)
    n_i, n_j = m // tm, f // tf
    row = lambda i, j: (i, 0)
    vec = lambda i, j: (0, 0)
    cast_next = next_weights is not None
    in_specs = [
        pl.BlockSpec((tm, d), row),
        pl.BlockSpec((1, d), vec),
        pl.BlockSpec((None, d, 2 * tf), lambda i, j: (j, 0, 0)),
        pl.BlockSpec((tf, d), lambda i, j: (j, 0)),
        pl.BlockSpec((1, d), vec),
        pl.BlockSpec((1, d), vec),
    ]
    args = [x, gain.reshape(1, d), wgu, wd, out_gain.reshape(1, d), mix_gain.reshape(1, d)]
    out_shape = [jax.ShapeDtypeStruct((m, d), F32)]
    out_specs = [pl.BlockSpec((tm, d), row)]
    if emit_mix:
        out_shape.append(jax.ShapeDtypeStruct((m, d), BF16))
        out_specs.append(pl.BlockSpec((tm, d), row))
    if cast_next:
        wg_n, wu_n, wd_n, layer = next_weights
        rg, rd = d // n_i, f // (n_i * n_j)
        in_specs += [
            pl.BlockSpec((None, rg, tf), lambda i, j: (layer, i, j)),
            pl.BlockSpec((None, rg, tf), lambda i, j: (layer, i, j)),
            pl.BlockSpec((None, rd, d), lambda i, j: (layer, i * n_j + j, 0)),
        ]
        args += [wg_n, wu_n, wd_n]
        out_shape += [jax.ShapeDtypeStruct((n_j, d, 2 * tf), BF16), jax.ShapeDtypeStruct((f, d), BF16)]
        out_specs += [pl.BlockSpec((None, rg, 2 * tf), lambda i, j: (j, i, 0)),
                      pl.BlockSpec((rd, d), lambda i, j: (i * n_j + j, 0))]
    res = pl.pallas_call(
        functools.partial(_ffn_kernel, post_norm=post_norm, emit_mix=emit_mix, cast_next=cast_next,
                          n_chunks=n_j),
        grid=(n_i, n_j),
        in_specs=in_specs,
        out_specs=out_specs,
        out_shape=out_shape,
        scratch_shapes=[] if emit_mix else [pltpu.VMEM((tm, d), BF16)],
        compiler_params=_cparams("parallel", "arbitrary"),
        name="ffn",
    )(*args)
    y = res[0]
    hm = res[1] if emit_mix else None
    nxt = tuple(res[-2:]) if cast_next else None
    return y, hm, nxt


def _qkv_kernel(h_ref, w_ref, seg_ref, gain_ref, o_ref, *, n_norm_blocks):
    j = pl.program_id(1)

    @pl.when(j < n_norm_blocks)
    def _():
        seg = seg_ref[...]
        h = h_ref[...]
        n_sub = w_ref.shape[1] // MXU_N
        cols = [slice(c * MXU_N, (c + 1) * MXU_N) for c in range(n_sub)]

        def normalise(a, sl):
            ss = _dot((a * a).astype(BF16), seg)
            y = a * lax.rsqrt(ss * (1.0 / NA_HEAD_DIM) + EPS) * gain_ref[:, sl]
            o_ref[:, sl] = y.astype(BF16)

        a_prev = _dot(h, w_ref[:, cols[0]])
        for c in range(1, n_sub):
            a_next = _dot(h, w_ref[:, cols[c]])
            normalise(a_prev, cols[c - 1])
            a_prev = a_next
        normalise(a_prev, cols[n_sub - 1])

    @pl.when(j >= n_norm_blocks)
    def _():
        o_ref[...] = _dot(h_ref[...], w_ref[...]).astype(BF16)


def _qkv(h, w, layer, gains):
    m, d = h.shape
    n = w.shape[2]
    tm = _pick(m, (1024, 512, 256, 128))
    tn = _pick(d, (1024, 512, 256))
    head = np.arange(MXU_N) // NA_HEAD_DIM
    seg = jnp.asarray(head[:, None] == head[None, :], BF16)
    return pl.pallas_call(
        functools.partial(_qkv_kernel, n_norm_blocks=2 * d // tn),
        grid=(m // tm, n // tn),
        in_specs=[
            pl.BlockSpec((tm, d), lambda i, j: (i, 0)),
            pl.BlockSpec((None, d, tn), lambda i, j: (layer, 0, j)),
            pl.BlockSpec((MXU_N, MXU_N), lambda i, j: (0, 0)),
            pl.BlockSpec((1, tn), lambda i, j: (0, j)),
        ],
        out_specs=pl.BlockSpec((tm, tn), lambda i, j: (i, j)),
        out_shape=jax.ShapeDtypeStruct((m, n), BF16),
        compiler_params=_cparams("parallel", "arbitrary"),
        name="na_qkv",
    )(h, w, seg, gains)


KEY_BLK = 32
QUERY_BLOCKS = ((0, 24, 0), (24, 16, 16), (40, 24, 32))
ATTN_ROWS = HEADS_PER_GROUP * GRID_W
ROWS_PER_ITER = 8


def _check_query_blocks():
    cols = np.arange(GRID_W)
    start = np.clip(cols - WIN_W // 2, 0, GRID_W - WIN_W)
    covered = 0
    for c0q, nq, c0k in QUERY_BLOCKS:
        assert c0q == covered and c0k % 16 == 0 and nq % 8 == 0
        assert (start[c0q:c0q + nq] >= c0k).all() and (start[c0q:c0q + nq] + WIN_W <= c0k + KEY_BLK).all()
        covered += nq
    assert covered == GRID_W


_check_query_blocks()


def _attn_rows(items, q_ref, k_ref, v_ref, t_ref, qm_ref, o_ref, *, rows):
    nt = (((1,), (1,)), ((), ()))

    def window(ref, rs, c0k):
        parts = []
        for i in range(WIN_H):
            start = rs * GRID_W + (i * GRID_W + c0k)
            if not isinstance(start, int):
                start = pl.multiple_of(start, 16)
            parts.append(ref[0, pl.ds(start, KEY_BLK), :])
        return jnp.concatenate(parts, axis=0)

    tiles = []
    for r_local, row, d in items:
        if isinstance(row, int):
            rs = min(max(row - WIN_H // 2, 0), rows - WIN_H)
        else:
            rs = jnp.clip(row - WIN_H // 2, 0, rows - WIN_H)
        qf = q_ref[0, pl.ds(_aligned(r_local, GRID_W), GRID_W), :].astype(F32)
        row_off = 0
        for c0q, nq, c0k in QUERY_BLOCKS:
            nrow = nq * HEADS_PER_GROUP
            rep = 1 if nq % 16 == 0 else 2
            qb = jnp.concatenate([qf[c0q:c0q + nq]] * rep, axis=0).astype(BF16)
            qs = jnp.tile(qb, (HEADS_PER_GROUP // rep, 1)) * qm_ref[row_off:row_off + nrow, :]
            s = lax.dot_general(qs, window(k_ref, rs, c0k), nt, preferred_element_type=F32)
            tiles.append((s, rs, d, row_off, nq, c0k))
            row_off += nrow

    probs = []
    for s, rs, d, row_off, nq, c0k in tiles:
        nrow = nq * HEADS_PER_GROUP
        bias_off = (WIN_H - d) * KEY_BLK
        s = s + t_ref[row_off:row_off + nrow, bias_off:bias_off + WIN_H * KEY_BLK]
        m = jnp.max(s, axis=-1, keepdims=True)
        p = jnp.exp(s - m)
        l = jnp.sum(p, axis=-1, keepdims=True)
        probs.append((p * (1.0 / l)).astype(BF16))

    outs = []
    for pn, (s, rs, d, row_off, nq, c0k) in zip(probs, tiles):
        o = _dot(pn, window(v_ref, rs, c0k))
        lane_head = lax.broadcasted_iota(jnp.int32, (nq, MXU_N), 1) // NA_HEAD_DIM
        acc = o[(HEADS_PER_GROUP - 1) * nq:]
        for hh in range(HEADS_PER_GROUP - 2, -1, -1):
            acc = jnp.where(lane_head == hh, o[hh * nq:(hh + 1) * nq], acc)
        outs.append(acc)

    nb = len(QUERY_BLOCKS)
    for n, (r_local, row, d) in enumerate(items):
        out = jnp.concatenate(outs[n * nb:(n + 1) * nb], axis=0)
        o_ref[0, pl.ds(_aligned(r_local, GRID_W), GRID_W), :] = out.astype(BF16)


def _attn_kernel(q_ref, k_ref, v_ref, t_ref, qm_ref, o_ref, *, rows, rows_per_step):
    half = WIN_H // 2
    rc = pl.program_id(2)
    n_rc = pl.num_programs(2)
    row0 = rc * rows_per_step
    body = functools.partial(_attn_rows, q_ref=q_ref, k_ref=k_ref, v_ref=v_ref, t_ref=t_ref,
                             qm_ref=qm_ref, o_ref=o_ref, rows=rows)
    edge = -(-half // ROWS_PER_ITER) * ROWS_PER_ITER
    lo = jnp.where(rc == 0, edge, 0)
    hi = jnp.where(rc == n_rc - 1, rows_per_step - edge, rows_per_step)

    def interior(it, carry):
        base = it * ROWS_PER_ITER
        body([(base + u, row0 + base + u, half) for u in range(ROWS_PER_ITER)])
        return carry

    lax.fori_loop(lo // ROWS_PER_ITER, hi // ROWS_PER_ITER, interior, 0)

    def window_offset(r):
        return r - min(max(r - half, 0), rows - WIN_H)

    def static_rows(first, local_shift):
        for r0 in range(first, first + edge, ROWS_PER_ITER):
            body([(r - local_shift, r, window_offset(r)) for r in range(r0, r0 + ROWS_PER_ITER)])

    @pl.when(rc == 0)
    def _():
        static_rows(0, 0)

    @pl.when(rc == n_rc - 1)
    def _():
        static_rows(rows - edge, rows - rows_per_step)


def _score_row_layout():
    h, c, c0 = [], [], []
    for c0q, nq, c0k in QUERY_BLOCKS:
        for hh in range(HEADS_PER_GROUP):
            h += [hh] * nq
            c += list(range(c0q, c0q + nq))
            c0 += [c0k] * nq
    return np.array(h), np.array(c), np.array(c0)


def _bias_table(rpb):
    n_dr = 2 * WIN_H - 1
    n_hg = rpb.shape[0] // HEADS_PER_GROUP
    parts = []
    for c0q, nq, c0k in QUERY_BLOCKS:
        c = np.arange(c0q, c0q + nq)
        start = np.clip(c - WIN_W // 2, 0, GRID_W - WIN_W)
        kc = c0k + np.arange(KEY_BLK)[None, :] + 0 * c[:, None]
        inside = (kc >= start[:, None]) & (kc < start[:, None] + WIN_W)
        dc = np.clip(kc - c[:, None] + WIN_W - 1, 0, 2 * WIN_W - 2)
        t = jnp.where(inside[None, None], rpb[:, :, dc], MASK_VALUE)
        t = jnp.transpose(t, (0, 2, 1, 3))
        parts.append(t.reshape(n_hg, HEADS_PER_GROUP * nq, n_dr * KEY_BLK))
    t = jnp.concatenate(parts, axis=1)
    return jnp.pad(t, ((0, 0), (0, 0), (KEY_BLK, 0)), constant_values=MASK_VALUE)


def _head_mask():
    h, _, _ = _score_row_layout()
    return jnp.asarray(h[:, None] == (np.arange(MXU_N) // NA_HEAD_DIM)[None, :], BF16)


def _attention(qkv, table, batch, seq, d):
    rows = seq // GRID_W
    rps = _pick(rows, (32, 16, 8))
    assert rps % ROWS_PER_ITER == 0 and rows >= 2 * WIN_H and rps >= 2 * ROWS_PER_ITER
    qkv3 = qkv.reshape(batch, seq, 3 * d)
    n_hg = d // MXU_N
    tw = table.shape[-1]
    out = pl.pallas_call(
        functools.partial(_attn_kernel, rows=rows, rows_per_step=rps),
        grid=(batch, n_hg, rows // rps),
        in_specs=[
            pl.BlockSpec((1, rps * GRID_W, MXU_N), lambda b, g, r: (b, r, g)),
            pl.BlockSpec((1, seq, MXU_N), lambda b, g, r: (b, 0, n_hg + g)),
            pl.BlockSpec((1, seq, MXU_N), lambda b, g, r: (b, 0, 2 * n_hg + g)),
            pl.BlockSpec((None, ATTN_ROWS, tw), lambda b, g, r: (g, 0, 0)),
            pl.BlockSpec((ATTN_ROWS, MXU_N), lambda b, g, r: (0, 0)),
        ],
        out_specs=pl.BlockSpec((1, rps * GRID_W, MXU_N), lambda b, g, r: (b, r, g)),
        out_shape=jax.ShapeDtypeStruct((batch, seq, d), BF16),
        compiler_params=_cparams("parallel", "parallel", "arbitrary"),
        name="na_attn",
    )(qkv3, qkv3, qkv3, table, _head_mask())
    return out.reshape(batch * seq, d)


def _proj_kernel(a_ref, w_ref, x_ref, o_ref):
    o_ref[...] = x_ref[...] + _dot(a_ref[...], w_ref[...])


def _proj_residual(a, w, layer, x):
    m, k = a.shape
    n = w.shape[2]
    tm = _pick(m, (512, 256, 128))
    tn = n
    return pl.pallas_call(
        _proj_kernel,
        grid=(m // tm, n // tn),
        in_specs=[
            pl.BlockSpec((tm, k), lambda i, j: (i, 0)),
            pl.BlockSpec((None, k, tn), lambda i, j: (layer, 0, j)),
            pl.BlockSpec((tm, tn), lambda i, j: (i, j)),
        ],
        out_specs=pl.BlockSpec((tm, tn), lambda i, j: (i, j)),
        out_shape=jax.ShapeDtypeStruct((m, n), F32),
        compiler_params=_cparams("parallel", "arbitrary"),
        name="proj_residual",
    )(a, w, x)


def _na_mixer(x1, hmix, w_qkv, q_gain, k_gain, rpb, w_o, layer, batch, seq):
    d = x1.shape[1]
    n_heads = d // NA_HEAD_DIM
    gains = jnp.concatenate([jnp.tile(q_gain * (NA_HEAD_DIM ** -0.5), n_heads),
                             jnp.tile(k_gain, n_heads),
                             jnp.ones((d,), F32)]).reshape(1, 3 * d)
    qkv = _qkv(hmix, w_qkv, layer, gains)
    att = _attention(qkv, _bias_table(rpb), batch, seq, d)
    return _proj_residual(att, w_o, layer, x1)


def _gelu_exact(x):
    return 0.5 * x * (1.0 + lax.erf(x * (1.0 / math.sqrt(2.0))))


def _sgu_in_kernel(h_ref, w_ref, z_ref, ss_ref, *, n_u_blocks):
    j = pl.program_id(1)
    half = h_ref.shape[0] // 2
    rows = (slice(0, half), slice(half, 2 * half))
    w = w_ref[...]
    acc = [_dot(h_ref[r, :], w) for r in rows]
    sq = []
    for r, a in zip(rows, acc):
        z = _gelu_exact(a)
        z_ref[r, :] = z.astype(BF16)
        sq.append(jnp.sum(z * z, axis=-1, keepdims=True))

    @pl.when(j == 0)
    def _():
        ss_ref[...] = jnp.zeros_like(ss_ref)

    @pl.when(j >= n_u_blocks)
    def _():
        for r, s in zip(rows, sq):
            ss_ref[r, :] += s


def _sgu_in(h, w, layer):
    m, d = h.shape
    n = w.shape[2]
    tm = _pick(m, (1024, 512, 256, 128))
    tn = _pick(n // 2, (1024, 512, 256, 128))
    return pl.pallas_call(
        functools.partial(_sgu_in_kernel, n_u_blocks=n // 2 // tn),
        grid=(m // tm, n // tn),
        in_specs=[
            pl.BlockSpec((tm, d), lambda i, j: (i, 0)),
            pl.BlockSpec((None, d, tn), lambda i, j: (layer, 0, j)),
        ],
        out_specs=[
            pl.BlockSpec((tm, tn), lambda i, j: (i, j)),
            pl.BlockSpec((tm, LANES), lambda i, j: (i, 0)),
        ],
        out_shape=[jax.ShapeDtypeStruct((m, n), BF16), jax.ShapeDtypeStruct((m, LANES), F32)],
        compiler_params=_cparams("parallel", "arbitrary"),
        name="sgu_in",
    )(h, w)


def _sgu_out_kernel(u_ref, v_ref, ss_ref, vg_ref, ws_ref, bs_ref, wo_ref, x_ref, o_ref, t_ref,
                    *, d_sgu):
    tm = u_ref.shape[0]
    gd = d_sgu // SGU_GROUPS
    rinv = lax.rsqrt(ss_ref[:, 0:1] * (1.0 / d_sgu) + EPS)
    for g in range(SGU_GROUPS):
        sl = slice(g * gd, (g + 1) * gd)
        vn = (v_ref[:, sl].astype(F32) * rinv * vg_ref[:, sl]).astype(BF16)
        ws = ws_ref[g]
        bs = bs_ref[g]
        for c in range(tm // CHUNK):
            rs = slice(c * CHUNK, (c + 1) * CHUNK)
            mixed = _dot(ws, vn[rs]) + bs
            t_ref[rs, sl] = (u_ref[rs, sl].astype(F32) * mixed).astype(BF16)
    o_ref[...] = x_ref[...] + _dot(t_ref[...], wo_ref[...])


def _sgu_out(z, ss, v_gain, w_s, b_s, w_out, layer, x):
    m, d = x.shape
    d_sgu = z.shape[1] // 2
    tm = _pick(m, (256, 128))
    const2 = lambda i: (0, 0)
    const3 = lambda i: (0, 0, 0)
    return pl.pallas_call(
        functools.partial(_sgu_out_kernel, d_sgu=d_sgu),
        grid=(m // tm,),
        in_specs=[
            pl.BlockSpec((tm, d_sgu), lambda i: (i, 0)),
            pl.BlockSpec((tm, d_sgu), lambda i: (i, 1)),
            pl.BlockSpec((tm, LANES), lambda i: (i, 0)),
            pl.BlockSpec((1, d_sgu), const2),
            pl.BlockSpec((SGU_GROUPS, CHUNK, CHUNK), const3),
            pl.BlockSpec((SGU_GROUPS, CHUNK, 1), const3),
            pl.BlockSpec((None, d_sgu, d), lambda i: (layer, 0, 0), pipeline_mode=pl.Buffered(1)),
            pl.BlockSpec((tm, d), lambda i: (i, 0)),
        ],
        out_specs=pl.BlockSpec((tm, d), lambda i: (i, 0)),
        out_shape=jax.ShapeDtypeStruct((m, d), F32),
        scratch_shapes=[pltpu.VMEM((tm, d_sgu), BF16)],
        compiler_params=_cparams("parallel"),
        name="sgu_out",
    )(z, z, ss, v_gain.reshape(1, d_sgu), w_s, b_s.reshape(SGU_GROUPS, CHUNK, 1), w_out, x)


def _sgu_mixer(x1, hmix, w_in, v_gain, w_s, b_s, w_out, layer):
    z, ss = _sgu_in(hmix, w_in, layer)
    return _sgu_out(z, ss, v_gain, w_s, b_s, w_out, layer, x1)


def _pool_kernel(xp_ref, xc_ref, xn_ref, g_ref, w_ref, sc_ref, o_ref, *, seq):
    tm, d = xc_ref.shape
    gd = d // len(POOL_WINDOWS)
    gain = g_ref[...]
    xc = xc_ref[...]
    hc = _rms(xc, gain)
    hext = jnp.concatenate([_rms(xp_ref[...], gain), hc, _rms(xn_ref[...], gain)], axis=0)
    hext = hext.astype(BF16)
    pos0 = (pl.program_id(0) * tm) % seq
    t = lax.broadcasted_iota(jnp.int32, (tm, tm + 2 * POOL_HALO), 0)
    off = lax.broadcasted_iota(jnp.int32, (tm, tm + 2 * POOL_HALO), 1) - POOL_HALO - t
    key_pos = pos0 + t + off
    in_seq = (key_pos >= 0) & (key_pos < seq)
    tq = pos0 + lax.broadcasted_iota(jnp.int32, (tm, 1), 0)
    for g, w in enumerate(POOL_WINDOWS):
        sl = slice(g * gd, (g + 1) * gd)
        band = ((off >= -(w // 2)) & (off < w // 2) & in_seq).astype(BF16)
        cnt = (jnp.clip(tq + w // 2, 0, seq) - jnp.clip(tq - w // 2, 0, seq)).astype(F32)
        p = _dot(band, hext[:, sl]) / cnt - hc[:, sl]
        y = _dot(p.astype(BF16), w_ref[g])
        o_ref[:, sl] = xc[:, sl] + y * sc_ref[:, sl]


def _pool_mixer(x, gain, w_grp, scale, seq):
    m, d = x.shape
    tm = _pick(seq, (512, 256, 128))
    nb = tm // POOL_HALO
    last = m // POOL_HALO - 1
    ng, gd, _ = w_grp.shape
    return pl.pallas_call(
        functools.partial(_pool_kernel, seq=seq),
        grid=(m // tm,),
        in_specs=[
            pl.BlockSpec((POOL_HALO, d), lambda i: (jnp.maximum(i * nb - 1, 0), 0)),
            pl.BlockSpec((tm, d), lambda i: (i, 0)),
            pl.BlockSpec((POOL_HALO, d), lambda i: (jnp.minimum((i + 1) * nb, last), 0)),
            pl.BlockSpec((1, d), lambda i: (0, 0)),
            pl.BlockSpec((ng, gd, gd), lambda i: (0, 0, 0)),
            pl.BlockSpec((1, d), lambda i: (0, 0)),
        ],
        out_specs=pl.BlockSpec((tm, d), lambda i: (i, 0)),
        out_shape=jax.ShapeDtypeStruct((m, d), F32),
        compiler_params=_cparams("parallel"),
        name="pool_mixer",
    )(x, x, x, gain.reshape(1, d), w_grp, scale.reshape(1, d))


def kernel(x, ffn1_norm, ffn1_w_gate, ffn1_w_up, ffn1_w_down, mix_norm, ffn2_norm, ffn2_w_gate, ffn2_w_up, ffn2_w_down, out_norm, na_w_qkv, na_q_gain, na_k_gain, na_rpb, na_w_o, sgu_w_in, sgu_v_gain, sgu_w_s, sgu_b_s, sgu_w_out, pool_w, pool_scale):
    batch, seq, d = x.shape
    depth = ffn1_norm.shape[0]
    f = ffn1_w_down.shape[1]
    tf = _ffn_chunk(f)
    w_qkv, w_o = _cast_bf16(na_w_qkv), _cast_bf16(na_w_o)
    w_in, w_out = _cast_bf16(sgu_w_in), _cast_bf16(sgu_w_out)
    xs = x.reshape(batch * seq, d)

    stacks = [(ffn1_w_gate, ffn1_w_up, ffn1_w_down), (ffn2_w_gate, ffn2_w_up, ffn2_w_down)]
    order = [(stacks[k], i) for i in range(depth) for k in range(2)]
    fuse_cast = _ffn_can_cast_next(batch * seq, d, f)

    def standalone_cast(n):
        (wg, wu, wd), layer = order[n]
        return _cast_gate_up(wg, wu, layer, tf), _cast_layer(wd, layer)

    def run_ffn(n, xin, gain, cur, **kw):
        nxt = None
        if fuse_cast and n + 1 < len(order):
            nxt = (*order[n + 1][0], order[n + 1][1])
        y, hm, cast = _ffn(xin, gain, cur[0], cur[1], out_norm[n // 2], mix_norm[n // 2],
                           next_weights=nxt, **kw)
        if cast is None and n + 1 < len(order):
            cast = standalone_cast(n + 1)
        return y, hm, cast

    cur = standalone_cast(0)
    for i in range(depth):
        kind, j = i % N_MIXERS, i // N_MIXERS
        x1, hmix, cur = run_ffn(2 * i, xs, ffn1_norm[i], cur, post_norm=False, emit_mix=kind != 2)
        if kind == 0:
            x2 = _na_mixer(x1, hmix, w_qkv, na_q_gain[j], na_k_gain[j], na_rpb[j], w_o, j, batch, seq)
        elif kind == 1:
            x2 = _sgu_mixer(x1, hmix, w_in, sgu_v_gain[j], sgu_w_s[j].astype(BF16), sgu_b_s[j],
                            w_out, j)
        else:
            x2 = _pool_mixer(x1, mix_norm[i], pool_w[j].astype(BF16), pool_scale[j], seq)
        xs, _, cur = run_ffn(2 * i + 1, x2, ffn2_norm[i], cur, post_norm=True, emit_mix=False)
    return xs.reshape(batch, seq, d)
```

```python
import functools
import math

import jax
import jax.numpy as jnp
import numpy as np
from jax import lax
from jax.experimental import pallas as pl
from jax.experimental.pallas import tpu as pltpu

EPS = 1e-6
N_MIXERS = 3
NA_HEAD_DIM = 32
GRID_W = 64
WIN_H = 8
WIN_W = 16
CHUNK = 128
SGU_GROUPS = 16
POOL_WINDOWS = (2, 4, 8, 16)
POOL_HALO = 64

LANES = 128
MXU_N = 256
HEADS_PER_GROUP = MXU_N // NA_HEAD_DIM
VMEM_LIMIT = 61440 * 1024
CAST_BLOCK_BYTES = 6 * 1024 * 1024
MASK_VALUE = -1e30

BF16 = jnp.bfloat16
F32 = jnp.float32


def _cparams(*sem):
    return pltpu.CompilerParams(dimension_semantics=sem, vmem_limit_bytes=VMEM_LIMIT)


def _rms(x, g):
    return x * lax.rsqrt(jnp.mean(x * x, axis=-1, keepdims=True) + EPS) * g


def _dot(a, b):
    return jnp.dot(a, b, preferred_element_type=F32)


def _aligned(i, m):
    return i * m if isinstance(i, int) else pl.multiple_of(i * m, m)


def _pick(n, prefs):
    for p in prefs:
        if n % p == 0:
            return p
    return n


def _cast_kernel(w_ref, o_ref):
    o_ref[...] = w_ref[...].astype(BF16)


def _cast_bf16(w):
    l, r, c = w.shape
    if l == 0:
        return w.astype(BF16)
    tr = r
    while tr * c * 4 > CAST_BLOCK_BYTES and tr % 2 == 0 and (tr // 2) % 16 == 0:
        tr //= 2
    return pl.pallas_call(
        _cast_kernel,
        grid=(l, r // tr),
        in_specs=[pl.BlockSpec((1, tr, c), lambda a, b: (a, b, 0))],
        out_specs=pl.BlockSpec((1, tr, c), lambda a, b: (a, b, 0)),
        out_shape=jax.ShapeDtypeStruct(w.shape, BF16),
        compiler_params=_cparams("parallel", "parallel"),
        name="cast_bf16",
    )(w)


def _cast_layer(w, layer):
    _, r, c = w.shape
    tr = r
    while tr * c * 4 > CAST_BLOCK_BYTES and tr % 2 == 0 and (tr // 2) % 16 == 0:
        tr //= 2
    return pl.pallas_call(
        _cast_kernel,
        grid=(r // tr,),
        in_specs=[pl.BlockSpec((None, tr, c), lambda b: (layer, b, 0))],
        out_specs=pl.BlockSpec((tr, c), lambda b: (b, 0)),
        out_shape=jax.ShapeDtypeStruct((r, c), BF16),
        compiler_params=_cparams("parallel"),
        name="cast_layer",
    )(w)


def _cast_pair(a_ref, b_ref, o_ref):
    n = a_ref.shape[-1]
    o_ref[:, :n] = a_ref[...].astype(BF16)
    o_ref[:, n:] = b_ref[...].astype(BF16)


def _cast_gate_up(wg, wu, layer, tf):
    _, d, f = wg.shape
    tr = _pick(d, (1024, 512, 256, 128))
    return pl.pallas_call(
        _cast_pair,
        grid=(f // tf, d // tr),
        in_specs=[pl.BlockSpec((None, tr, tf), lambda j, r: (layer, r, j)),
                  pl.BlockSpec((None, tr, tf), lambda j, r: (layer, r, j))],
        out_specs=pl.BlockSpec((None, tr, 2 * tf), lambda j, r: (j, r, 0)),
        out_shape=jax.ShapeDtypeStruct((f // tf, d, 2 * tf), BF16),
        compiler_params=_cparams("parallel", "parallel"),
        name="cast_gate_up",
    )(wg, wu)


def _ffn_kernel(x_ref, g_ref, wgu_ref, wd_ref, og_ref, mg_ref, *rest,
                post_norm, emit_mix, cast_next, n_chunks):
    rest = list(rest)
    next_f32 = [rest.pop(0) for _ in range(3)] if cast_next else None
    y_ref = rest.pop(0)
    if emit_mix:
        hm_ref = rest.pop(0)
        h_ref = hm_ref
    next_bf16 = [rest.pop(0) for _ in range(2)] if cast_next else None
    if not emit_mix:
        h_ref = rest.pop(0)
    j = pl.program_id(1)

    def cast_next_weights():
        wg_n, wu_n, wd_n = next_f32
        wgu_o, wd_o = next_bf16
        _cast_pair(wg_n, wu_n, wgu_o)
        wd_o[...] = wd_n[...].astype(BF16)

    def chunk():
        tf = wd_ref.shape[0]
        gu = _dot(h_ref[...], wgu_ref[...])
        gate, up = gu[:, :tf], gu[:, tf:]
        act = (gate * jax.nn.sigmoid(gate) * up * 0.5).astype(BF16)
        return _dot(act, wd_ref[...])

    def step(first, last):
        if cast_next:
            cast_next_weights()
        if first:
            h_ref[...] = _rms(x_ref[...], g_ref[...]).astype(BF16)
        y = (x_ref[...] if first else y_ref[...]) + chunk()
        if last and post_norm:
            y = _rms(y, og_ref[...])
        y_ref[...] = y
        if last and emit_mix:
            hm_ref[...] = _rms(y, mg_ref[...]).astype(BF16)

    if n_chunks == 1:
        step(True, True)
    else:
        pl.when(j == 0)(lambda: step(True, False))
        pl.when(j == n_chunks - 1)(lambda: step(False, True))
        if n_chunks > 2:
            pl.when((j > 0) & (j < n_chunks - 1))(lambda: step(False, False))


def _ffn_chunk(f):
    return _pick(f, (512, 256, 128))


def _ffn_tiles(m, f):
    return _pick(m, (1024, 512, 256, 128)), _ffn_chunk(f)


def _ffn_can_cast_next(m, d, f):
    tm, tf = _ffn_tiles(m, f)
    n_i, n_j = m // tm, f // tf
    return d % (16 * n_i) == 0 and f % (16 * n_i * n_j) == 0


def _ffn(x, gain, wgu, wd, out_gain, mix_gain, *, post_norm, emit_mix, next_weights=None):
    m, d = x.shape
    f = wd.shape[0]
    tm, tf = _ffn_tiles(m, f)
    n_i, n_j = m // tm, f // tf
    row = lambda i, j: (i, 0)
    vec = lambda i, j: (0, 0)
    cast_next = next_weights is not None
    in_specs = [
        pl.BlockSpec((tm, d), row),
        pl.BlockSpec((1, d), vec),
        pl.BlockSpec((None, d, 2 * tf), lambda i, j: (j, 0, 0)),
        pl.BlockSpec((tf, d), lambda i, j: (j, 0)),
        pl.BlockSpec((1, d), vec),
        pl.BlockSpec((1, d), vec),
    ]
    args = [x, gain.reshape(1, d), wgu, wd, out_gain.reshape(1, d), mix_gain.reshape(1, d)]
    out_shape = [jax.ShapeDtypeStruct((m, d), F32)]
    out_specs = [pl.BlockSpec((tm, d), row)]
    if emit_mix:
        out_shape.append(jax.ShapeDtypeStruct((m, d), BF16))
        out_specs.append(pl.BlockSpec((tm, d), row))
    if cast_next:
        wg_n, wu_n, wd_n, layer = next_weights
        rg, rd = d // n_i, f // (n_i * n_j)
        in_specs += [
            pl.BlockSpec((None, rg, tf), lambda i, j: (layer, i, j)),
            pl.BlockSpec((None, rg, tf), lambda i, j: (layer, i, j)),
            pl.BlockSpec((None, rd, d), lambda i, j: (layer, i * n_j + j, 0)),
        ]
        args += [wg_n, wu_n, wd_n]
        out_shape += [jax.ShapeDtypeStruct((n_j, d, 2 * tf), BF16), jax.ShapeDtypeStruct((f, d), BF16)]
        out_specs += [pl.BlockSpec((None, rg, 2 * tf), lambda i, j: (j, i, 0)),
                      pl.BlockSpec((rd, d), lambda i, j: (i * n_j + j, 0))]
    res = pl.pallas_call(
        functools.partial(_ffn_kernel, post_norm=post_norm, emit_mix=emit_mix, cast_next=cast_next,
                          n_chunks=n_j),
        grid=(n_i, n_j),
        in_specs=in_specs,
        out_specs=out_specs,
        out_shape=out_shape,
        scratch_shapes=[] if emit_mix else [pltpu.VMEM((tm, d), BF16)],
        compiler_params=_cparams("parallel", "arbitrary"),
        name="ffn",
    )(*args)
    y = res[0]
    hm = res[1] if emit_mix else None
    nxt = tuple(res[-2:]) if cast_next else None
    return y, hm, nxt


def _qkv_kernel(h_ref, w_ref, seg_ref, gain_ref, o_ref, *, n_norm_blocks):
    j = pl.program_id(1)

    @pl.when(j < n_norm_blocks)
    def _():
        seg = seg_ref[...]
        h = h_ref[...]
        n_sub = w_ref.shape[1] // MXU_N
        cols = [slice(c * MXU_N, (c + 1) * MXU_N) for c in range(n_sub)]

        def normalise(a, sl):
            ss = _dot((a * a).astype(BF16), seg)
            y = a * lax.rsqrt(ss * (1.0 / NA_HEAD_DIM) + EPS) * gain_ref[:, sl]
            o_ref[:, sl] = y.astype(BF16)

        a_prev = _dot(h, w_ref[:, cols[0]])
        for c in range(1, n_sub):
            a_next = _dot(h, w_ref[:, cols[c]])
            normalise(a_prev, cols[c - 1])
            a_prev = a_next
        normalise(a_prev, cols[n_sub - 1])

    @pl.when(j >= n_norm_blocks)
    def _():
        o_ref[...] = _dot(h_ref[...], w_ref[...]).astype(BF16)


def _qkv(h, w, layer, gains):
    m, d = h.shape
    n = w.shape[2]
    tm = _pick(m, (1024, 512, 256, 128))
    tn = _pick(d, (2048, 1024, 512, 256))
    head = np.arange(MXU_N) // NA_HEAD_DIM
    seg = jnp.asarray(head[:, None] == head[None, :], BF16)
    return pl.pallas_call(
        functools.partial(_qkv_kernel, n_norm_blocks=2 * d // tn),
        grid=(m // tm, n // tn),
        in_specs=[
            pl.BlockSpec((tm, d), lambda i, j: (i, 0)),
            pl.BlockSpec((None, d, tn), lambda i, j: (layer, 0, j)),
            pl.BlockSpec((MXU_N, MXU_N), lambda i, j: (0, 0)),
            pl.BlockSpec((1, tn), lambda i, j: (0, j)),
        ],
        out_specs=pl.BlockSpec((tm, tn), lambda i, j: (i, j)),
        out_shape=jax.ShapeDtypeStruct((m, n), BF16),
        compiler_params=_cparams("parallel", "arbitrary"),
        name="na_qkv",
    )(h, w, seg, gains)


KEY_BLK = 32
QUERY_BLOCKS = ((0, 24, 0), (24, 16, 16), (40, 24, 32))
ATTN_ROWS = HEADS_PER_GROUP * GRID_W
ROWS_PER_ITER = 8


def _check_query_blocks():
    cols = np.arange(GRID_W)
    start = np.clip(cols - WIN_W // 2, 0, GRID_W - WIN_W)
    covered = 0
    for c0q, nq, c0k in QUERY_BLOCKS:
        assert c0q == covered and c0k % 16 == 0 and nq % 8 == 0
        assert (start[c0q:c0q + nq] >= c0k).all() and (start[c0q:c0q + nq] + WIN_W <= c0k + KEY_BLK).all()
        covered += nq
    assert covered == GRID_W


_check_query_blocks()


def _attn_rows(items, q_ref, k_ref, v_ref, t_ref, qm_ref, o_ref, *, rows):
    nt = (((1,), (1,)), ((), ()))

    def window(ref, rs, c0k):
        parts = []
        for i in range(WIN_H):
            start = rs * GRID_W + (i * GRID_W + c0k)
            if not isinstance(start, int):
                start = pl.multiple_of(start, 16)
            parts.append(ref[0, pl.ds(start, KEY_BLK), :])
        return jnp.concatenate(parts, axis=0)

    tiles = []
    for r_local, row, d in items:
        if isinstance(row, int):
            rs = min(max(row - WIN_H // 2, 0), rows - WIN_H)
        else:
            rs = jnp.clip(row - WIN_H // 2, 0, rows - WIN_H)
        qf = q_ref[0, pl.ds(_aligned(r_local, GRID_W), GRID_W), :].astype(F32)
        row_off = 0
        for c0q, nq, c0k in QUERY_BLOCKS:
            nrow = nq * HEADS_PER_GROUP
            rep = 1 if nq % 16 == 0 else 2
            qb = jnp.concatenate([qf[c0q:c0q + nq]] * rep, axis=0).astype(BF16)
            qs = jnp.tile(qb, (HEADS_PER_GROUP // rep, 1)) * qm_ref[row_off:row_off + nrow, :]
            s = lax.dot_general(qs, window(k_ref, rs, c0k), nt, preferred_element_type=F32)
            tiles.append((s, rs, d, row_off, nq, c0k))
            row_off += nrow

    probs = []
    for s, rs, d, row_off, nq, c0k in tiles:
        nrow = nq * HEADS_PER_GROUP
        bias_off = (WIN_H - d) * KEY_BLK
        s = s + t_ref[row_off:row_off + nrow, bias_off:bias_off + WIN_H * KEY_BLK]
        m = jnp.max(s, axis=-1, keepdims=True)
        p = jnp.exp(s - m)
        l = jnp.sum(p, axis=-1, keepdims=True)
        probs.append((p * (1.0 / l)).astype(BF16))

    outs = []
    for pn, (s, rs, d, row_off, nq, c0k) in zip(probs, tiles):
        o = _dot(pn, window(v_ref, rs, c0k))
        lane_head = lax.broadcasted_iota(jnp.int32, (nq, MXU_N), 1) // NA_HEAD_DIM
        acc = o[(HEADS_PER_GROUP - 1) * nq:]
        for hh in range(HEADS_PER_GROUP - 2, -1, -1):
            acc = jnp.where(lane_head == hh, o[hh * nq:(hh + 1) * nq], acc)
        outs.append(acc)

    nb = len(QUERY_BLOCKS)
    for n, (r_local, row, d) in enumerate(items):
        out = jnp.concatenate(outs[n * nb:(n + 1) * nb], axis=0)
        o_ref[0, pl.ds(_aligned(r_local, GRID_W), GRID_W), :] = out.astype(BF16)


def _attn_kernel(q_ref, k_ref, v_ref, t_ref, qm_ref, o_ref, *, rows, rows_per_step):
    half = WIN_H // 2
    rc = pl.program_id(2)
    n_rc = pl.num_programs(2)
    row0 = rc * rows_per_step
    body = functools.partial(_attn_rows, q_ref=q_ref, k_ref=k_ref, v_ref=v_ref, t_ref=t_ref,
                             qm_ref=qm_ref, o_ref=o_ref, rows=rows)
    edge = -(-half // ROWS_PER_ITER) * ROWS_PER_ITER
    lo = jnp.where(rc == 0, edge, 0)
    hi = jnp.where(rc == n_rc - 1, rows_per_step - edge, rows_per_step)

    def interior(it, carry):
        base = it * ROWS_PER_ITER
        body([(base + u, row0 + base + u, half) for u in range(ROWS_PER_ITER)])
        return carry

    lax.fori_loop(lo // ROWS_PER_ITER, hi // ROWS_PER_ITER, interior, 0)

    def window_offset(r):
        return r - min(max(r - half, 0), rows - WIN_H)

    def static_rows(first, local_shift):
        for r0 in range(first, first + edge, ROWS_PER_ITER):
            body([(r - local_shift, r, window_offset(r)) for r in range(r0, r0 + ROWS_PER_ITER)])

    @pl.when(rc == 0)
    def _():
        static_rows(0, 0)

    @pl.when(rc == n_rc - 1)
    def _():
        static_rows(rows - edge, rows - rows_per_step)


def _score_row_layout():
    h, c, c0 = [], [], []
    for c0q, nq, c0k in QUERY_BLOCKS:
        for hh in range(HEADS_PER_GROUP):
            h += [hh] * nq
            c += list(range(c0q, c0q + nq))
            c0 += [c0k] * nq
    return np.array(h), np.array(c), np.array(c0)


def _bias_table(rpb):
    n_dr = 2 * WIN_H - 1
    n_hg = rpb.shape[0] // HEADS_PER_GROUP
    parts = []
    for c0q, nq, c0k in QUERY_BLOCKS:
        c = np.arange(c0q, c0q + nq)
        start = np.clip(c - WIN_W // 2, 0, GRID_W - WIN_W)
        kc = c0k + np.arange(KEY_BLK)[None, :] + 0 * c[:, None]
        inside = (kc >= start[:, None]) & (kc < start[:, None] + WIN_W)
        dc = np.clip(kc - c[:, None] + WIN_W - 1, 0, 2 * WIN_W - 2)
        t = jnp.where(inside[None, None], rpb[:, :, dc], MASK_VALUE)
        t = jnp.transpose(t, (0, 2, 1, 3))
        parts.append(t.reshape(n_hg, HEADS_PER_GROUP * nq, n_dr * KEY_BLK))
    t = jnp.concatenate(parts, axis=1)
    return jnp.pad(t, ((0, 0), (0, 0), (KEY_BLK, 0)), constant_values=MASK_VALUE)


def _head_mask():
    h, _, _ = _score_row_layout()
    return jnp.asarray(h[:, None] == (np.arange(MXU_N) // NA_HEAD_DIM)[None, :], BF16)


def _attention(qkv, table, batch, seq, d):
    rows = seq // GRID_W
    rps = _pick(rows, (64, 32, 16))
    assert rps % ROWS_PER_ITER == 0 and rows >= 2 * WIN_H and rps >= 2 * ROWS_PER_ITER
    qkv3 = qkv.reshape(batch, seq, 3 * d)
    n_hg = d // MXU_N
    tw = table.shape[-1]
    out = pl.pallas_call(
        functools.partial(_attn_kernel, rows=rows, rows_per_step=rps),
        grid=(batch, n_hg, rows // rps),
        in_specs=[
            pl.BlockSpec((1, rps * GRID_W, MXU_N), lambda b, g, r: (b, r, g)),
            pl.BlockSpec((1, seq, MXU_N), lambda b, g, r: (b, 0, n_hg + g)),
            pl.BlockSpec((1, seq, MXU_N), lambda b, g, r: (b, 0, 2 * n_hg + g)),
            pl.BlockSpec((None, ATTN_ROWS, tw), lambda b, g, r: (g, 0, 0)),
            pl.BlockSpec((ATTN_ROWS, MXU_N), lambda b, g, r: (0, 0)),
        ],
        out_specs=pl.BlockSpec((1, rps * GRID_W, MXU_N), lambda b, g, r: (b, r, g)),
        out_shape=jax.ShapeDtypeStruct((batch, seq, d), BF16),
        compiler_params=_cparams("parallel", "parallel", "arbitrary"),
        name="na_attn",
    )(qkv3, qkv3, qkv3, table, _head_mask())
    return out.reshape(batch * seq, d)


def _proj_kernel(a_ref, w_ref, x_ref, o_ref):
    o_ref[...] = x_ref[...] + _dot(a_ref[...], w_ref[...])


def _proj_residual(a, w, layer, x):
    m, k = a.shape
    n = w.shape[2]
    tm = _pick(m, (512, 256, 128))
    tn = n
    return pl.pallas_call(
        _proj_kernel,
        grid=(m // tm, n // tn),
        in_specs=[
            pl.BlockSpec((tm, k), lambda i, j: (i, 0)),
            pl.BlockSpec((None, k, tn), lambda i, j: (layer, 0, j)),
            pl.BlockSpec((tm, tn), lambda i, j: (i, j)),
        ],
        out_specs=pl.BlockSpec((tm, tn), lambda i, j: (i, j)),
        out_shape=jax.ShapeDtypeStruct((m, n), F32),
        compiler_params=_cparams("parallel", "arbitrary"),
        name="proj_residual",
    )(a, w, x)


def _na_mixer(x1, hmix, w_qkv, q_gain, k_gain, rpb, w_o, layer, batch, seq):
    d = x1.shape[1]
    n_heads = d // NA_HEAD_DIM
    gains = jnp.concatenate([jnp.tile(q_gain * (NA_HEAD_DIM ** -0.5), n_heads),
                             jnp.tile(k_gain, n_heads),
                             jnp.ones((d,), F32)]).reshape(1, 3 * d)
    qkv = _qkv(hmix, w_qkv, layer, gains)
    att = _attention(qkv, _bias_table(rpb), batch, seq, d)
    return _proj_residual(att, w_o, layer, x1)


def _gelu_exact(x):
    return 0.5 * x * (1.0 + lax.erf(x * (1.0 / math.sqrt(2.0))))


def _sgu_in_kernel(h_ref, w_ref, z_ref, ss_ref, *, n_u_blocks):
    j = pl.program_id(1)
    half = h_ref.shape[0] // 2
    rows = (slice(0, half), slice(half, 2 * half))
    w = w_ref[...]
    acc = [_dot(h_ref[r, :], w) for r in rows]
    sq = []
    for r, a in zip(rows, acc):
        z = _gelu_exact(a)
        z_ref[r, :] = z.astype(BF16)
        sq.append(jnp.sum(z * z, axis=-1, keepdims=True))

    @pl.when(j == 0)
    def _():
        ss_ref[...] = jnp.zeros_like(ss_ref)

    @pl.when(j >= n_u_blocks)
    def _():
        for r, s in zip(rows, sq):
            ss_ref[r, :] += s


def _sgu_in(h, w, layer):
    m, d = h.shape
    n = w.shape[2]
    tm = _pick(m, (1024, 512, 256, 128))
    tn = _pick(n // 2, (2048, 1024, 512, 256, 128))
    return pl.pallas_call(
        functools.partial(_sgu_in_kernel, n_u_blocks=n // 2 // tn),
        grid=(m // tm, n // tn),
        in_specs=[
            pl.BlockSpec((tm, d), lambda i, j: (i, 0)),
            pl.BlockSpec((None, d, tn), lambda i, j: (layer, 0, j)),
        ],
        out_specs=[
            pl.BlockSpec((tm, tn), lambda i, j: (i, j)),
            pl.BlockSpec((tm, LANES), lambda i, j: (i, 0)),
        ],
        out_shape=[jax.ShapeDtypeStruct((m, n), BF16), jax.ShapeDtypeStruct((m, LANES), F32)],
        compiler_params=_cparams("parallel", "arbitrary"),
        name="sgu_in",
    )(h, w)


def _sgu_out_kernel(u_ref, v_ref, ss_ref, vg_ref, ws_ref, bs_ref, wo_ref, x_ref, o_ref, t_ref,
                    *, d_sgu):
    tm = u_ref.shape[0]
    gd = d_sgu // SGU_GROUPS
    rinv = lax.rsqrt(ss_ref[:, 0:1] * (1.0 / d_sgu) + EPS)
    for g in range(SGU_GROUPS):
        sl = slice(g * gd, (g + 1) * gd)
        vn = (v_ref[:, sl].astype(F32) * rinv * vg_ref[:, sl]).astype(BF16)
        ws = ws_ref[g]
        bs = bs_ref[g]
        for c in range(tm // CHUNK):
            rs = slice(c * CHUNK, (c + 1) * CHUNK)
            mixed = _dot(ws, vn[rs]) + bs
            t_ref[rs, sl] = (u_ref[rs, sl].astype(F32) * mixed).astype(BF16)
    o_ref[...] = x_ref[...] + _dot(t_ref[...], wo_ref[...])


def _sgu_out(z, ss, v_gain, w_s, b_s, w_out, layer, x):
    m, d = x.shape
    d_sgu = z.shape[1] // 2
    tm = _pick(m, (512, 256, 128))
    const2 = lambda i: (0, 0)
    const3 = lambda i: (0, 0, 0)
    return pl.pallas_call(
        functools.partial(_sgu_out_kernel, d_sgu=d_sgu),
        grid=(m // tm,),
        in_specs=[
            pl.BlockSpec((tm, d_sgu), lambda i: (i, 0)),
            pl.BlockSpec((tm, d_sgu), lambda i: (i, 1)),
            pl.BlockSpec((tm, LANES), lambda i: (i, 0)),
            pl.BlockSpec((1, d_sgu), const2),
            pl.BlockSpec((SGU_GROUPS, CHUNK, CHUNK), const3),
            pl.BlockSpec((SGU_GROUPS, CHUNK, 1), const3),
            pl.BlockSpec((None, d_sgu, d), lambda i: (layer, 0, 0), pipeline_mode=pl.Buffered(1)),
            pl.BlockSpec((tm, d), lambda i: (i, 0)),
        ],
        out_specs=pl.BlockSpec((tm, d), lambda i: (i, 0)),
        out_shape=jax.ShapeDtypeStruct((m, d), F32),
        scratch_shapes=[pltpu.VMEM((tm, d_sgu), BF16)],
        compiler_params=_cparams("parallel"),
        name="sgu_out",
    )(z, z, ss, v_gain.reshape(1, d_sgu), w_s, b_s.reshape(SGU_GROUPS, CHUNK, 1), w_out, x)


def _sgu_mixer(x1, hmix, w_in, v_gain, w_s, b_s, w_out, layer):
    z, ss = _sgu_in(hmix, w_in, layer)
    return _sgu_out(z, ss, v_gain, w_s, b_s, w_out, layer, x1)


def _pool_kernel(xp_ref, xc_ref, xn_ref, g_ref, w_ref, sc_ref, o_ref, *, seq):
    tm, d = xc_ref.shape
    gd = d // len(POOL_WINDOWS)
    gain = g_ref[...]
    xc = xc_ref[...]
    hc = _rms(xc, gain)
    hext = jnp.concatenate([_rms(xp_ref[...], gain), hc, _rms(xn_ref[...], gain)], axis=0)
    hext = hext.astype(BF16)
    pos0 = (pl.program_id(0) * tm) % seq
    t = lax.broadcasted_iota(jnp.int32, (tm, tm + 2 * POOL_HALO), 0)
    off = lax.broadcasted_iota(jnp.int32, (tm, tm + 2 * POOL_HALO), 1) - POOL_HALO - t
    key_pos = pos0 + t + off
    in_seq = (key_pos >= 0) & (key_pos < seq)
    tq = pos0 + lax.broadcasted_iota(jnp.int32, (tm, 1), 0)
    for g, w in enumerate(POOL_WINDOWS):
        sl = slice(g * gd, (g + 1) * gd)
        band = ((off >= -(w // 2)) & (off < w // 2) & in_seq).astype(BF16)
        cnt = (jnp.clip(tq + w // 2, 0, seq) - jnp.clip(tq - w // 2, 0, seq)).astype(F32)
        p = _dot(band, hext[:, sl]) / cnt - hc[:, sl]
        y = _dot(p.astype(BF16), w_ref[g])
        o_ref[:, sl] = xc[:, sl] + y * sc_ref[:, sl]


def _pool_mixer(x, gain, w_grp, scale, seq):
    m, d = x.shape
    tm = _pick(seq, (512, 256, 128))
    nb = tm // POOL_HALO
    last = m // POOL_HALO - 1
    ng, gd, _ = w_grp.shape
    return pl.pallas_call(
        functools.partial(_pool_kernel, seq=seq),
        grid=(m // tm,),
        in_specs=[
            pl.BlockSpec((POOL_HALO, d), lambda i: (jnp.maximum(i * nb - 1, 0), 0)),
            pl.BlockSpec((tm, d), lambda i: (i, 0)),
            pl.BlockSpec((POOL_HALO, d), lambda i: (jnp.minimum((i + 1) * nb, last), 0)),
            pl.BlockSpec((1, d), lambda i: (0, 0)),
            pl.BlockSpec((ng, gd, gd), lambda i: (0, 0, 0)),
            pl.BlockSpec((1, d), lambda i: (0, 0)),
        ],
        out_specs=pl.BlockSpec((tm, d), lambda i: (i, 0)),
        out_shape=jax.ShapeDtypeStruct((m, d), F32),
        compiler_params=_cparams("parallel"),
        name="pool_mixer",
    )(x, x, x, gain.reshape(1, d), w_grp, scale.reshape(1, d))


def kernel(x, ffn1_norm, ffn1_w_gate, ffn1_w_up, ffn1_w_down, mix_norm, ffn2_norm, ffn2_w_gate, ffn2_w_up, ffn2_w_down, out_norm, na_w_qkv, na_q_gain, na_k_gain, na_rpb, na_w_o, sgu_w_in, sgu_v_gain, sgu_w_s, sgu_b_s, sgu_w_out, pool_w, pool_scale):
    batch, seq, d = x.shape
    depth = ffn1_norm.shape[0]
    f = ffn1_w_down.shape[1]
    tf = _ffn_chunk(f)
    w_qkv, w_o = _cast_bf16(na_w_qkv), _cast_bf16(na_w_o)
    w_in, w_out = _cast_bf16(sgu_w_in), _cast_bf16(sgu_w_out)
    xs = x.reshape(batch * seq, d)

    stacks = [(ffn1_w_gate, ffn1_w_up, ffn1_w_down), (ffn2_w_gate, ffn2_w_up, ffn2_w_down)]
    order = [(stacks[k], i) for i in range(depth) for k in range(2)]
    fuse_cast = _ffn_can_cast_next(batch * seq, d, f)

    def standalone_cast(n):
        (wg, wu, wd), layer = order[n]
        return _cast_gate_up(wg, wu, layer, tf), _cast_layer(wd, layer)

    def run_ffn(n, xin, gain, cur, **kw):
        nxt = None
        if fuse_cast and n + 1 < len(order):
            nxt = (*order[n + 1][0], order[n + 1][1])
        y, hm, cast = _ffn(xin, gain, cur[0], cur[1], out_norm[n // 2], mix_norm[n // 2],
                           next_weights=nxt, **kw)
        if cast is None and n + 1 < len(order):
            cast = standalone_cast(n + 1)
        return y, hm, cast

    cur = standalone_cast(0)
    for i in range(depth):
        kind, j = i % N_MIXERS, i // N_MIXERS
        x1, hmix, cur = run_ffn(2 * i, xs, ffn1_norm[i], cur, post_norm=False, emit_mix=kind != 2)
        if kind == 0:
            x2 = _na_mixer(x1, hmix, w_qkv, na_q_gain[j], na_k_gain[j], na_rpb[j], w_o, j, batch, seq)
        elif kind == 1:
            x2 = _sgu_mixer(x1, hmix, w_in, sgu_v_gain[j], sgu_w_s[j].astype(BF16), sgu_b_s[j],
                            w_out, j)
        else:
            x2 = _pool_mixer(x1, mix_norm[i], pool_w[j].astype(BF16), pool_scale[j], seq)
        xs, _, cur = run_ffn(2 * i + 1, x2, ffn2_norm[i], cur, post_norm=True, emit_mix=False)
    return xs.reshape(batch, seq, d)
```
